```python
import math
import jax, jax.numpy as jnp
from jax import lax
import numpy as np

D_MODEL = 1024
BATCH = 2
SEQ = 8192
DEPTH = 2

CHUNK = 64
Q_BLOCK = 128
N_A_LAYERS = DEPTH // 2
N_B_LAYERS = DEPTH - N_A_LAYERS
DIFF_HEADS = 8
DIFF_HEAD_DIM = D_MODEL // (2 * DIFF_HEADS)
DIFF_V_DIM = 2 * DIFF_HEAD_DIM
SB_HEADS = 16
SB_HEAD_DIM = D_MODEL // SB_HEADS
PEER_HEADS = 8
PEER_N_KEYS = 128
PEER_N_EXPERTS = PEER_N_KEYS * PEER_N_KEYS
PEER_TOPK = 16
PEER_QUERY_DIM = 128
PEER_HALF = PEER_QUERY_DIM // 2
PEER_TOKEN_BLOCK = 128
ROPE_THETA = 10000.0
LN_EPS = 1e-5
RMS_EPS = 1e-5
DEEPNORM_ALPHA = (2.0 * DEPTH) ** 0.25
DEEPNORM_BETA = (8.0 * DEPTH) ** -0.25

kernel_name = "yoco_diffattn_stickbreak_peer_deepnorm"


def _layer_norm(x, g, b):
    xf = x.astype(jnp.float32)
    mu = jnp.mean(xf, axis=-1, keepdims=True)
    var = jnp.mean(jnp.square(xf - mu), axis=-1, keepdims=True)
    y = (xf - mu) * lax.rsqrt(var + LN_EPS)
    return (y * g.astype(jnp.float32) + b.astype(jnp.float32)).astype(x.dtype)


def _rope_tables(seq):
    pos = jnp.arange(seq, dtype=jnp.float32)
    inv = ROPE_THETA ** (-jnp.arange(0, DIFF_HEAD_DIM, 2, dtype=jnp.float32) / DIFF_HEAD_DIM)
    ang = pos[:, None] * inv[None, :]
    ang = jnp.concatenate([ang, ang], axis=-1)
    return jnp.cos(ang), jnp.sin(ang)


def _rope(t, cos, sin):
    half = t.shape[-1] // 2
    rot = jnp.concatenate([-t[..., half:], t[..., :half]], axis=-1)
    return (t * cos[None, :, None, :] + rot * sin[None, :, None, :]).astype(t.dtype)


def _diff_attention(x, w_qkv, w_o, lam_params, subln_g, lambda_init, cos, sin):
    B, S, D = x.shape
    nb = S // Q_BLOCK
    qkv = x @ w_qkv
    q, k, v = qkv[..., :D], qkv[..., D:2 * D], qkv[..., 2 * D:]
    q = _rope(q.reshape(B, S, 2 * DIFF_HEADS, DIFF_HEAD_DIM), cos, sin)
    k = _rope(k.reshape(B, S, 2 * DIFF_HEADS, DIFF_HEAD_DIM), cos, sin)
    v = v.reshape(B, S, DIFF_HEADS, DIFF_V_DIM)
    kh = k.reshape(B, S, DIFF_HEADS, 2, DIFF_HEAD_DIM).transpose(0, 2, 3, 1, 4)
    vh = v.transpose(0, 2, 1, 3)
    qb = q.reshape(B, nb, Q_BLOCK, DIFF_HEADS, 2, DIFF_HEAD_DIM).transpose(1, 0, 3, 4, 2, 5)
    lp = lam_params.astype(jnp.float32)
    lam = jnp.exp(jnp.sum(lp[0] * lp[1])) - jnp.exp(jnp.sum(lp[2] * lp[3])) + lambda_init
    key_chunk = jnp.arange(S) // CHUNK
    scale = DIFF_HEAD_DIM ** -0.5

    def block(args):
        qblk, i = args
        t = i * Q_BLOCK + jnp.arange(Q_BLOCK)
        mask = key_chunk[None, :] <= (t // CHUNK)[:, None]
        s = jnp.einsum('bhmqd,bhmkd->bhmqk', qblk, kh).astype(jnp.float32) * scale
        p = jax.nn.softmax(jnp.where(mask, s, -jnp.inf), axis=-1)
        a = p[:, :, 0] - lam * p[:, :, 1]
        return jnp.einsum('bhqk,bhkd->bhqd', a.astype(vh.dtype), vh)

    o = lax.map(block, (qb, jnp.arange(nb)))
    of = o.astype(jnp.float32)
    of = of * lax.rsqrt(jnp.mean(of * of, axis=-1, keepdims=True) + RMS_EPS)
    of = of * subln_g.astype(jnp.float32) * (1.0 - lambda_init)
    o = of.astype(x.dtype).transpose(1, 0, 3, 2, 4).reshape(B, S, DIFF_HEADS * DIFF_V_DIM)
    return o @ w_o


def _shared_kv(x, w_kv):
    B, S, D = x.shape
    kv = x @ w_kv
    k = kv[..., :D].reshape(B, S, SB_HEADS, SB_HEAD_DIM).transpose(0, 2, 1, 3)
    v = kv[..., D:].reshape(B, S, SB_HEADS, SB_HEAD_DIM).transpose(0, 2, 1, 3)
    return k, v


def _stick_breaking(x, w_q, w_o, k_sb, v_sb):
    B, S, D = x.shape
    nb = S // Q_BLOCK
    q = (x @ w_q).reshape(B, nb, Q_BLOCK, SB_HEADS, SB_HEAD_DIM).transpose(1, 0, 3, 2, 4)
    key_pos = jnp.arange(S)
    scale = SB_HEAD_DIM ** -0.5

    def block(args):
        qblk, i = args
        t = i * Q_BLOCK + jnp.arange(Q_BLOCK)
        mask = key_pos[None, :] < t[:, None]
        z = jnp.einsum('bhqd,bhkd->bhqk', qblk, k_sb).astype(jnp.float32) * scale
        log_one_minus = jnp.where(mask, jax.nn.log_sigmoid(-z), 0.0)
        tail = lax.cumsum(log_one_minus, axis=3, reverse=True) - log_one_minus
        a = jnp.where(mask, jnp.exp(jax.nn.log_sigmoid(z) + tail), 0.0)
        return jnp.einsum('bhqk,bhkd->bhqd', a.astype(v_sb.dtype), v_sb)

    o = lax.map(block, (q, jnp.arange(nb)))
    o = o.transpose(1, 0, 3, 2, 4).reshape(B, S, SB_HEADS * SB_HEAD_DIM)
    return o @ w_o


def _peer(x, w_pq, sub_keys, u, v):
    B, S, D = x.shape
    T = B * S
    xt = x.reshape(T, D)
    q = (xt @ w_pq).reshape(T, PEER_HEADS, 2, PEER_HALF)
    scores = jnp.einsum('thcd,hckd->thck', q, sub_keys).astype(jnp.float32)
    s_top, i_top = lax.top_k(scores, PEER_TOPK)
    cand = s_top[:, :, 0, :, None] + s_top[:, :, 1, None, :]
    cand_ids = i_top[:, :, 0, :, None] * PEER_N_KEYS + i_top[:, :, 1, None, :]
    g_s, g_i = lax.top_k(cand.reshape(T, PEER_HEADS, PEER_TOPK * PEER_TOPK), PEER_TOPK)
    expert_ids = jnp.take_along_axis(cand_ids.reshape(T, PEER_HEADS, PEER_TOPK * PEER_TOPK), g_i, axis=-1)
    gates = jax.nn.softmax(g_s, axis=-1).astype(x.dtype)
    nblk = T // PEER_TOKEN_BLOCK
    n_sel = PEER_HEADS * PEER_TOPK
    ids_b = expert_ids.reshape(nblk, PEER_TOKEN_BLOCK, n_sel)
    g_b = gates.reshape(nblk, PEER_TOKEN_BLOCK, n_sel)
    x_b = xt.reshape(nblk, PEER_TOKEN_BLOCK, D)

    def block(args):
        xblk, idb, gb = args
        u_sel = jnp.take(u, idb, axis=0)
        h = jax.nn.gelu(jnp.einsum('td,ted->te', xblk, u_sel), approximate=False)
        v_sel = jnp.take(v, idb, axis=0)
        return jnp.einsum('te,ted->td', gb * h, v_sel)

    out = lax.map(block, (x_b, ids_b, g_b))
    return out.reshape(B, S, D)


def setup_inputs(seed: int = 0) -> dict:
    key = jax.random.key(seed)
    ks = jax.random.split(key, 16)
    D = D_MODEL
    f32 = jnp.float32
    nrm = lambda k, shape, s: jax.random.normal(k, shape, f32) * s
    v_col_scale = jnp.concatenate([jnp.ones((2 * D,), f32), jnp.full((D,), DEEPNORM_BETA, f32)])
    kv_col_scale = jnp.concatenate([jnp.ones((D,), f32), jnp.full((D,), DEEPNORM_BETA, f32)])
    return {
        "x": nrm(ks[0], (BATCH, SEQ, D), 1.0),
        "ln_g": 1.0 + nrm(ks[1], (DEPTH, 2, D), 0.02),
        "ln_b": nrm(ks[2], (DEPTH, 2, D), 0.02),
        "w_qkv_a": nrm(ks[3], (N_A_LAYERS, D, 3 * D), D ** -0.5) * v_col_scale,
        "w_o_a": nrm(ks[4], (N_A_LAYERS, DIFF_HEADS * DIFF_V_DIM, D), D ** -0.5 * DEEPNORM_BETA),
        "lambda_qk_a": nrm(ks[5], (N_A_LAYERS, 4, DIFF_HEAD_DIM), 0.1),
        "subln_g_a": 1.0 + nrm(ks[6], (N_A_LAYERS, DIFF_V_DIM), 0.02),
        "w_kv_b": nrm(ks[7], (D, 2 * D), D ** -0.5) * kv_col_scale,
        "w_q_b": nrm(ks[8], (N_B_LAYERS, D, SB_HEADS * SB_HEAD_DIM), D ** -0.5),
        "w_o_b": nrm(ks[9], (N_B_LAYERS, SB_HEADS * SB_HEAD_DIM, D), D ** -0.5 * DEEPNORM_BETA),
        "peer_w_q": nrm(ks[10], (DEPTH, D, PEER_HEADS * PEER_QUERY_DIM), D ** -0.5),
        "peer_sub_keys": nrm(ks[11], (DEPTH, PEER_HEADS, 2, PEER_N_KEYS, PEER_HALF), PEER_HALF ** -0.5),
        "peer_u": nrm(ks[12], (DEPTH, PEER_N_EXPERTS, D), D ** -0.5),
        "peer_v": nrm(ks[13], (DEPTH, PEER_N_EXPERTS, D), DEEPNORM_BETA * PEER_HEADS ** -0.5),
    }


def reference(x, ln_g, ln_b, w_qkv_a, w_o_a, lambda_qk_a, subln_g_a, w_kv_b, w_q_b, w_o_b,
              peer_w_q, peer_sub_keys, peer_u, peer_v):
    S = x.shape[1]
    cos, sin = _rope_tables(S)
    k_sb = None
    v_sb = None
    for layer in range(DEPTH):
        if layer < N_A_LAYERS:
            lambda_init = 0.8 - 0.6 * math.exp(-0.3 * layer)
            mix = _diff_attention(x, w_qkv_a[layer], w_o_a[layer], lambda_qk_a[layer],
                                  subln_g_a[layer], lambda_init, cos, sin)
        else:
            if layer == N_A_LAYERS:
                k_sb, v_sb = _shared_kv(x, w_kv_b)
            j = layer - N_A_LAYERS
            mix = _stick_breaking(x, w_q_b[j], w_o_b[j], k_sb, v_sb)
        x = _layer_norm(DEEPNORM_ALPHA * x + mix, ln_g[layer, 0], ln_b[layer, 0])
        ffn = _peer(x, peer_w_q[layer], peer_sub_keys[layer], peer_u[layer], peer_v[layer])
        x = _layer_norm(DEEPNORM_ALPHA * x + ffn, ln_g[layer, 1], ln_b[layer, 1])
    return x
```

```python
import functools
import math

import jax
import jax.numpy as jnp
from jax import lax
from jax.experimental import pallas as pl
from jax.experimental.pallas import tpu as pltpu

F32 = jnp.float32
BF16 = jnp.bfloat16

DIFF_HEADS = 8
DIFF_HEAD_DIM = 64
SB_HEADS = 16
SB_HEAD_DIM = 64
PEER_HEADS = 8
PEER_N_KEYS = 128
PEER_TOPK = 16
PEER_HALF = 64
CHUNK = 64
ROPE_THETA = 10000.0
LN_EPS = 1e-5
RMS_EPS = 1e-5
DEPTH = 2
DEEPNORM_ALPHA = (2.0 * DEPTH) ** 0.25

LANES = 128

PROJ_TM = 256
ATTN_TQ = 256
ATTN_TK = 256
SB_TQ = 256
SB_TK = 128
ROUTE_TM = 256
DENSE_TM = 512
DENSE_TE = 1024
VMEM_LIMIT = 56 * 1024 * 1024


def _cparams(sem):
    return pltpu.CompilerParams(dimension_semantics=sem, vmem_limit_bytes=VMEM_LIMIT)


def _proj_kernel(x_ref, w_ref, cos_ref, sina_ref, sinb_ref, o_ref, *, rope_groups, scale_groups, scale):
    x = x_ref[...].astype(BF16)
    acc = jnp.dot(x, w_ref[...], preferred_element_type=F32)
    n_groups = acc.shape[1] // LANES
    for g in range(n_groups):
        t = acc[:, g * LANES:(g + 1) * LANES]
        if g < rope_groups:
            t = (t * cos_ref[...] + pltpu.roll(t, LANES - 32, 1) * sina_ref[...]
                 + pltpu.roll(t, 32, 1) * sinb_ref[...])
        if g < scale_groups:
            t = t * scale
        o_ref[:, g * LANES:(g + 1) * LANES] = t.astype(o_ref.dtype)


def _proj(x2d, w_bf16, tables, *, seq, rope_groups, scale_groups, scale):
    T, D = x2d.shape
    N = w_bf16.shape[1]
    tm = PROJ_TM
    n_seq_blocks = seq // tm
    cos, sina, sinb = tables
    tab_spec = pl.BlockSpec((tm, LANES), lambda i: (i % n_seq_blocks, 0))
    return pl.pallas_call(
        functools.partial(_proj_kernel, rope_groups=rope_groups, scale_groups=scale_groups, scale=scale),
        grid=(T // tm,),
        in_specs=[pl.BlockSpec((tm, D), lambda i: (i, 0)),
                  pl.BlockSpec((D, N), lambda i: (0, 0)),
                  tab_spec, tab_spec, tab_spec],
        out_specs=pl.BlockSpec((tm, N), lambda i: (i, 0)),
        out_shape=jax.ShapeDtypeStruct((T, N), BF16),
        compiler_params=_cparams(("parallel",)),
        name="proj",
    )(x2d, w_bf16, cos, sina, sinb)


def _rope_tables(seq):
    pos = jnp.arange(seq, dtype=F32)
    inv = ROPE_THETA ** (-jnp.arange(0, DIFF_HEAD_DIM, 2, dtype=F32) / DIFF_HEAD_DIM)
    ang = pos[:, None] * inv[None, :]
    ang = jnp.concatenate([ang, ang, ang, ang], axis=-1)
    cos, sin = jnp.cos(ang), jnp.sin(ang)
    first_half = (jnp.arange(LANES) % DIFF_HEAD_DIM) < (DIFF_HEAD_DIM // 2)
    sina = jnp.where(first_half[None, :], -sin, 0.0)
    sinb = jnp.where(first_half[None, :], 0.0, sin)
    return cos, sina, sinb


def _diff_attn_kernel(lp_ref, g_ref, q_ref, k_ref, v_ref, o_ref, m_ref, l_ref, acc_ref, *,
                      tq, tk, lambda_init):
    i = pl.program_id(2)
    q = q_ref[...]
    lane = lax.broadcasted_iota(jnp.int32, q.shape, 1)
    zero = jnp.zeros_like(q)
    qs = jnp.concatenate([jnp.where(lane < DIFF_HEAD_DIM, q, zero),
                          jnp.where(lane >= DIFF_HEAD_DIM, q, zero)], axis=0)

    m_ref[...] = jnp.full(m_ref.shape, -jnp.inf, F32)
    l_ref[...] = jnp.zeros(l_ref.shape, F32)
    acc_ref[...] = jnp.zeros(acc_ref.shape, F32)

    def step(j, masked):
        start = pl.multiple_of(j * tk, tk)
        kj = k_ref[pl.ds(start, tk), :]
        vj = v_ref[pl.ds(start, tk), :]
        s = lax.dot_general(qs, kj, (((1,), (1,)), ((), ())), preferred_element_type=F32)
        if masked:
            row = lax.broadcasted_iota(jnp.int32, s.shape, 0) % tq
            col = lax.broadcasted_iota(jnp.int32, s.shape, 1)
            s = jnp.where((col // CHUNK) <= (row // CHUNK), s, -jnp.inf)
        m_prev = m_ref[...]
        m_new = jnp.maximum(m_prev, jnp.max(s, axis=1, keepdims=True))
        alpha = jnp.exp(m_prev - m_new)
        p = jnp.exp(s - m_new)
        l_ref[...] = alpha * l_ref[...] + jnp.sum(p, axis=1, keepdims=True)
        acc_ref[...] = alpha * acc_ref[...] + jnp.dot(p.astype(BF16), vj, preferred_element_type=F32)
        m_ref[...] = m_new

    def body(j, carry):
        step(j, False)
        return carry

    lax.fori_loop(0, i, body, 0)
    step(i, True)

    lp = lp_ref[...]
    s01 = jnp.sum(lp[0:1, :] * lp[1:2, :], axis=1, keepdims=True)
    s23 = jnp.sum(lp[2:3, :] * lp[3:4, :], axis=1, keepdims=True)
    lam = jnp.exp(s01) - jnp.exp(s23) + lambda_init
    o0 = acc_ref[0:tq, :] / l_ref[0:tq, :]
    o1 = acc_ref[tq:2 * tq, :] / l_ref[tq:2 * tq, :]
    o = o0 - lam * o1
    o = o * lax.rsqrt(jnp.mean(o * o, axis=1, keepdims=True) + RMS_EPS)
    o = o * g_ref[...] * (1.0 - lambda_init)
    o_ref[...] = o.astype(o_ref.dtype)


def _diff_attention(qkv, lam_params, subln_g, *, batch, seq, lambda_init):
    T = qkv.shape[0]
    tq, tk = ATTN_TQ, ATTN_TK
    assert tq == tk
    nq = seq // tq
    H = DIFF_HEADS
    return pl.pallas_call(
        functools.partial(_diff_attn_kernel, tq=tq, tk=tk, lambda_init=lambda_init),
        grid=(batch, H, nq),
        in_specs=[pl.BlockSpec((4, DIFF_HEAD_DIM), lambda b, h, i: (0, 0)),
                  pl.BlockSpec((1, LANES), lambda b, h, i: (0, 0)),
                  pl.BlockSpec((tq, LANES), lambda b, h, i: (b * nq + i, h)),
                  pl.BlockSpec((seq, LANES), lambda b, h, i: (b, H + h)),
                  pl.BlockSpec((seq, LANES), lambda b, h, i: (b, 2 * H + h))],
        out_specs=pl.BlockSpec((tq, LANES), lambda b, h, i: (b * nq + i, h)),
        out_shape=jax.ShapeDtypeStruct((T, H * LANES), BF16),
        scratch_shapes=[pltpu.VMEM((2 * tq, 1), F32), pltpu.VMEM((2 * tq, 1), F32),
                        pltpu.VMEM((2 * tq, LANES), F32)],
        compiler_params=_cparams(("parallel", "parallel", "arbitrary")),
        name="diff_attn",
    )(lam_params, subln_g.reshape(1, LANES), qkv, qkv, qkv)


def _stick_kernel(q_ref, k_ref, v_ref, o_ref, r_ref, acc_ref, *, tq, tk):
    i = pl.program_id(2)
    q = q_ref[...]
    lane = lax.broadcasted_iota(jnp.int32, q.shape, 1)
    zero = jnp.zeros_like(q)
    qs = jnp.concatenate([jnp.where(lane < SB_HEAD_DIM, q, zero),
                          jnp.where(lane >= SB_HEAD_DIM, q, zero)], axis=0)

    kk = lax.broadcasted_iota(jnp.int32, (2 * tk, tk), 0) % tk
    jj = lax.broadcasted_iota(jnp.int32, (2 * tk, tk), 1)
    suffix = jnp.where(kk > jj, 1.0, 0.0).astype(BF16)

    r_ref[...] = jnp.zeros(r_ref.shape, F32)
    acc_ref[...] = jnp.zeros(acc_ref.shape, F32)

    def step(jb, masked):
        start = pl.multiple_of(jb * tk, tk)
        kj = k_ref[pl.ds(start, tk), :]
        vj = v_ref[pl.ds(start, tk), :]
        z = lax.dot_general(qs, kj, (((1,), (1,)), ((), ())), preferred_element_type=F32)
        lp = jnp.log(1.0 + jnp.exp(-jnp.abs(z)))
        lsm = -(jnp.maximum(z, 0.0) + lp)
        ls = jnp.minimum(z, 0.0) - lp
        if masked:
            row = i * tq + lax.broadcasted_iota(jnp.int32, z.shape, 0) % tq
            col = jb * tk + lax.broadcasted_iota(jnp.int32, z.shape, 1)
            keep = col < row
            lsm = jnp.where(keep, lsm, 0.0)
        hi = lsm.astype(BF16)
        lo = (lsm - hi.astype(F32)).astype(BF16)
        tail = jnp.dot(jnp.concatenate([hi, lo], axis=1), suffix, preferred_element_type=F32)
        a = jnp.exp(ls + tail + r_ref[...])
        if masked:
            a = jnp.where(keep, a, 0.0)
        r_ref[...] = r_ref[...] + jnp.sum(lsm, axis=1, keepdims=True)
        acc_ref[...] = acc_ref[...] + jnp.dot(a.astype(BF16), vj, preferred_element_type=F32)

    n_diag = tq // tk
    first_diag = i * n_diag
    for d in range(n_diag - 1, -1, -1):
        step(first_diag + d, True)

    def body(t, carry):
        step(first_diag - 1 - t, False)
        return carry

    lax.fori_loop(0, first_diag, body, 0)

    out = jnp.where(lane < SB_HEAD_DIM, acc_ref[0:tq, :], acc_ref[tq:2 * tq, :])
    o_ref[...] = out.astype(o_ref.dtype)


def _stick_breaking(qkv, *, batch, seq):
    T = qkv.shape[0]
    tq, tk = SB_TQ, SB_TK
    nq = seq // tq
    P = SB_HEADS // 2
    return pl.pallas_call(
        functools.partial(_stick_kernel, tq=tq, tk=tk),
        grid=(batch, P, nq),
        in_specs=[pl.BlockSpec((tq, LANES), lambda b, p, i: (b * nq + i, p)),
                  pl.BlockSpec((seq, LANES), lambda b, p, i: (b, P + p)),
                  pl.BlockSpec((seq, LANES), lambda b, p, i: (b, 2 * P + p))],
        out_specs=pl.BlockSpec((tq, LANES), lambda b, p, i: (b * nq + i, p)),
        out_shape=jax.ShapeDtypeStruct((T, P * LANES), BF16),
        scratch_shapes=[pltpu.VMEM((2 * tq, 1), F32), pltpu.VMEM((2 * tq, LANES), F32)],
        compiler_params=_cparams(("parallel", "parallel", "arbitrary")),
        name="stick_breaking",
    )(qkv, qkv, qkv)


def _layer_norm_rows(y, g, b):
    mu = jnp.mean(y, axis=1, keepdims=True)
    d = y - mu
    var = jnp.mean(d * d, axis=1, keepdims=True)
    return d * lax.rsqrt(var + LN_EPS) * g + b


def _proj_ln_kernel(a_ref, w_ref, x_ref, g_ref, b_ref, o_ref):
    y = jnp.dot(a_ref[...], w_ref[...], preferred_element_type=F32) + DEEPNORM_ALPHA * x_ref[...]
    o_ref[...] = _layer_norm_rows(y, g_ref[...], b_ref[...])


def _proj_ln(a_bf16, w_bf16, x2d, g, b):
    T, D = x2d.shape
    tm = PROJ_TM
    return pl.pallas_call(
        _proj_ln_kernel,
        grid=(T // tm,),
        in_specs=[pl.BlockSpec((tm, D), lambda i: (i, 0)),
                  pl.BlockSpec((D, D), lambda i: (0, 0)),
                  pl.BlockSpec((tm, D), lambda i: (i, 0)),
                  pl.BlockSpec((1, D), lambda i: (0, 0)),
                  pl.BlockSpec((1, D), lambda i: (0, 0))],
        out_specs=pl.BlockSpec((tm, D), lambda i: (i, 0)),
        out_shape=jax.ShapeDtypeStruct((T, D), F32),
        compiler_params=_cparams(("parallel",)),
        name="proj_ln",
    )(a_bf16, w_bf16, x2d, g.reshape(1, D), b.reshape(1, D))


N_TOP = PEER_TOPK + 1


def _extract_top(scores, n, with_rank):
    work = scores
    vals = []
    rank = jnp.full(scores.shape, float(PEER_N_KEYS), F32) if with_rank else None
    for r in range(n):
        m = jnp.max(work, axis=0, keepdims=True)
        vals.append(m)
        hit = work == m
        if with_rank:
            rank = jnp.where(hit, float(r), rank)
        work = jnp.where(hit, -jnp.inf, work)
    return vals, rank


def _route_kernel(x_ref, w_ref, kp_ref, rb_ref, e2_ref, n_ref, a1_ref):
    x = x_ref[...].astype(BF16)
    qp = jnp.dot(x, w_ref[...], preferred_element_type=F32).astype(BF16)
    tm = x.shape[0]
    for h in range(PEER_HEADS):
        qh = qp[:, h * LANES:(h + 1) * LANES]
        nt = (((1,), (1,)), ((), ()))
        s1 = lax.dot_general(kp_ref[h, 0], qh, nt, preferred_element_type=F32)
        s2 = lax.dot_general(kp_ref[h, 1], qh, nt, preferred_element_type=F32)
        top1, _ = _extract_top(s1, N_TOP, False)
        top2, rank2 = _extract_top(s2, N_TOP, True)
        cands = [top1[i] + top2[j] for i in range(N_TOP) for j in range(N_TOP)
                 if (i + 1) * (j + 1) <= N_TOP]
        pad = (-len(cands)) % 8
        cands += [jnp.full((1, tm), -jnp.inf, F32)] * pad
        ctop, _ = _extract_top(jnp.concatenate(cands, axis=0), N_TOP, False)
        cut = 0.5 * (ctop[PEER_TOPK - 1] + ctop[PEER_TOPK])
        cmax = ctop[0]
        z = ctop[0] - cmax
        z = jnp.exp(z)
        for r in range(1, PEER_TOPK):
            z = z + jnp.exp(ctop[r] - cmax)
        count1 = jnp.zeros(s1.shape, F32)
        for j in range(PEER_TOPK):
            count1 = count1 + jnp.where(s1 + top2[j] >= cut, 1.0, 0.0)
        rb_ref[h] = rank2
        e2_ref[h] = jnp.exp(s2 - top2[0])
        n_ref[h] = count1
        a1_ref[h] = jnp.exp(s1 - top1[0]) / z


def _peer_route(x2d, w_pq_bf16, keys_padded):
    T, D = x2d.shape
    tm = ROUTE_TM
    out_spec = pl.BlockSpec((PEER_HEADS, PEER_N_KEYS, tm), lambda i: (0, 0, i))
    out_sds = jax.ShapeDtypeStruct((PEER_HEADS, PEER_N_KEYS, T), F32)
    return pl.pallas_call(
        _route_kernel,
        grid=(T // tm,),
        in_specs=[pl.BlockSpec((tm, D), lambda i: (i, 0)),
                  pl.BlockSpec((D, PEER_HEADS * LANES), lambda i: (0, 0)),
                  pl.BlockSpec((PEER_HEADS, 2, PEER_N_KEYS, LANES), lambda i: (0, 0, 0, 0))],
        out_specs=[out_spec] * 4,
        out_shape=[out_sds] * 4,
        compiler_params=_cparams(("parallel",)),
        name="peer_route",
    )(x2d, w_pq_bf16, keys_padded)


def _peer_dense_kernel(x_ref, u_ref, vt_ref, rb_ref, e2_ref, n_ref, a1_ref, g_ref, b_ref, o_ref,
                       xt_ref, acc_ref, gs_ref, *, te):
    e = pl.program_id(1)

    @pl.when(e == 0)
    def _():
        xt_ref[...] = x_ref[...].T.astype(BF16)
        acc_ref[...] = jnp.zeros(acc_ref.shape, F32)

    ht = jnp.dot(u_ref[...], xt_ref[...], preferred_element_type=F32)
    for i in range(te // PEER_N_KEYS):
        hi = ht[i * PEER_N_KEYS:(i + 1) * PEER_N_KEYS, :]
        w = jnp.zeros(hi.shape, F32)
        for h in range(PEER_HEADS):
            cnt = n_ref[h, i:i + 1, :]
            a1 = a1_ref[h, i:i + 1, :]
            w = w + a1 * jnp.where(rb_ref[h] < cnt, e2_ref[h], 0.0)
        gelu = 0.5 * hi * (1.0 + lax.erf(hi * (1.0 / math.sqrt(2.0))))
        gs_ref[i * PEER_N_KEYS:(i + 1) * PEER_N_KEYS, :] = (w * gelu).astype(BF16)
    acc_ref[...] += jnp.dot(vt_ref[...], gs_ref[...], preferred_element_type=F32)

    @pl.when(e == pl.num_programs(1) - 1)
    def _():
        y = acc_ref[...].T + DEEPNORM_ALPHA * x_ref[...]
        o_ref[...] = _layer_norm_rows(y, g_ref[...], b_ref[...])


def _peer_dense(x2d, u_bf16, vt_bf16, routing, g, b):
    T, D = x2d.shape
    E = u_bf16.shape[0]
    tm, te = DENSE_TM, DENSE_TE
    rb, e2, cnt, a1 = routing
    full_spec = pl.BlockSpec((PEER_HEADS, PEER_N_KEYS, tm), lambda t, e: (0, 0, t))
    row_spec = pl.BlockSpec((PEER_HEADS, te // PEER_N_KEYS, tm), lambda t, e: (0, e, t))
    return pl.pallas_call(
        functools.partial(_peer_dense_kernel, te=te),
        grid=(T // tm, E // te),
        in_specs=[pl.BlockSpec((tm, D), lambda t, e: (t, 0)),
                  pl.BlockSpec((te, D), lambda t, e: (e, 0)),
                  pl.BlockSpec((D, te), lambda t, e: (0, e)),
                  full_spec, full_spec, row_spec, row_spec,
                  pl.BlockSpec((1, D), lambda t, e: (0, 0)),
                  pl.BlockSpec((1, D), lambda t, e: (0, 0))],
        out_specs=pl.BlockSpec((tm, D), lambda t, e: (t, 0)),
        out_shape=jax.ShapeDtypeStruct((T, D), F32),
        scratch_shapes=[pltpu.VMEM((D, tm), BF16), pltpu.VMEM((D, tm), F32), pltpu.VMEM((te, tm), BF16)],
        compiler_params=_cparams(("parallel", "arbitrary")),
        name="peer_dense",
    )(x2d, u_bf16, vt_bf16, rb, e2, cnt, a1, g.reshape(1, D), b.reshape(1, D))


def _pad_sub_keys(sub_keys):
    z = jnp.zeros_like(sub_keys[:, 0])
    first = jnp.concatenate([sub_keys[:, 0], z], axis=-1)
    second = jnp.concatenate([z, sub_keys[:, 1]], axis=-1)
    return jnp.stack([first, second], axis=1).astype(BF16)


def _peer(x2d, w_pq, sub_keys, u, v, g, b):
    routing = _peer_route(x2d, w_pq.astype(BF16), _pad_sub_keys(sub_keys))
    return _peer_dense(x2d, u.astype(BF16), v.T.astype(BF16), routing, g, b)


def kernel(x, ln_g, ln_b, w_qkv_a, w_o_a, lambda_qk_a, subln_g_a, w_kv_b, w_q_b, w_o_b,
           peer_w_q, peer_sub_keys, peer_u, peer_v):
    B, S, D = x.shape
    x2d = x.reshape(B * S, D)
    tables = _rope_tables(S)
    n_qk_groups = 2 * D // LANES

    lambda_init = 0.8 - 0.6 * math.exp(-0.3 * 0)
    qkv = _proj(x2d, w_qkv_a[0].astype(BF16), tables, seq=S, rope_groups=n_qk_groups,
                scale_groups=D // LANES, scale=DIFF_HEAD_DIM ** -0.5)
    att = _diff_attention(qkv, lambda_qk_a[0], subln_g_a[0], batch=B, seq=S, lambda_init=lambda_init)
    x2d = _proj_ln(att, w_o_a[0].astype(BF16), x2d, ln_g[0, 0], ln_b[0, 0])
    x2d = _peer(x2d, peer_w_q[0], peer_sub_keys[0], peer_u[0], peer_v[0], ln_g[0, 1], ln_b[0, 1])

    w_qkv_b = jnp.concatenate([w_q_b[0], w_kv_b], axis=1).astype(BF16)
    qkv = _proj(x2d, w_qkv_b, tables, seq=S, rope_groups=0,
                scale_groups=D // LANES, scale=SB_HEAD_DIM ** -0.5)
    sb = _stick_breaking(qkv, batch=B, seq=S)
    x2d = _proj_ln(sb, w_o_b[0].astype(BF16), x2d, ln_g[1, 0], ln_b[1, 0])
    x2d = _peer(x2d, peer_w_q[1], peer_sub_keys[1], peer_u[1], peer_v[1], ln_g[1, 1], ln_b[1, 1])
    return x2d.reshape(B, S, D)
```

```python
import functools
import math

import jax
import jax.numpy as jnp
from jax import lax
from jax.experimental import pallas as pl
from jax.experimental.pallas import tpu as pltpu

F32 = jnp.float32
BF16 = jnp.bfloat16

DIFF_HEADS = 8
DIFF_HEAD_DIM = 64
SB_HEADS = 16
SB_HEAD_DIM = 64
PEER_HEADS = 8
PEER_N_KEYS = 128
PEER_TOPK = 16
PEER_HALF = 64
CHUNK = 64
ROPE_THETA = 10000.0
LN_EPS = 1e-5
RMS_EPS = 1e-5
DEPTH = 2
DEEPNORM_ALPHA = (2.0 * DEPTH) ** 0.25

LANES = 128
SUM_ROWS = 16
LOG2E = math.log2(math.e)

PROJ_TM = 256
ATTN_TQ = 512
ATTN_TK = 256
SB_TQ = 512
SB_TK = 256
ROUTE_TM = 256
DENSE_TM = 512
DENSE_TE = 1024
VMEM_LIMIT = 56 * 1024 * 1024


def _cparams(sem):
    return pltpu.CompilerParams(dimension_semantics=sem, vmem_limit_bytes=VMEM_LIMIT)


def _proj_kernel(x_ref, w_ref, cos_ref, sina_ref, sinb_ref, o_ref, *, rope_groups, scale_groups, scale):
    x = x_ref[...].astype(BF16)
    acc = jnp.dot(x, w_ref[...], preferred_element_type=F32)
    n_groups = acc.shape[1] // LANES
    for g in range(n_groups):
        t = acc[:, g * LANES:(g + 1) * LANES]
        if g < rope_groups:
            t = (t * cos_ref[...] + pltpu.roll(t, LANES - 32, 1) * sina_ref[...]
                 + pltpu.roll(t, 32, 1) * sinb_ref[...])
        if g < scale_groups:
            t = t * scale
        o_ref[:, g * LANES:(g + 1) * LANES] = t.astype(o_ref.dtype)


def _proj(x2d, w_bf16, tables, *, seq, rope_groups, scale_groups, scale):
    T, D = x2d.shape
    N = w_bf16.shape[1]
    tm = PROJ_TM
    n_seq_blocks = seq // tm
    cos, sina, sinb = tables
    tab_spec = pl.BlockSpec((tm, LANES), lambda i: (i % n_seq_blocks, 0))
    return pl.pallas_call(
        functools.partial(_proj_kernel, rope_groups=rope_groups, scale_groups=scale_groups, scale=scale),
        grid=(T // tm,),
        in_specs=[pl.BlockSpec((tm, D), lambda i: (i, 0)),
                  pl.BlockSpec((D, N), lambda i: (0, 0)),
                  tab_spec, tab_spec, tab_spec],
        out_specs=pl.BlockSpec((tm, N), lambda i: (i, 0)),
        out_shape=jax.ShapeDtypeStruct((T, N), BF16),
        compiler_params=_cparams(("parallel",)),
        name="proj",
    )(x2d, w_bf16, cos, sina, sinb)


def _rope_tables(seq):
    pos = jnp.arange(seq, dtype=F32)
    inv = ROPE_THETA ** (-jnp.arange(0, DIFF_HEAD_DIM, 2, dtype=F32) / DIFF_HEAD_DIM)
    ang = pos[:, None] * inv[None, :]
    ang = jnp.concatenate([ang, ang, ang, ang], axis=-1)
    cos, sin = jnp.cos(ang), jnp.sin(ang)
    first_half = (jnp.arange(LANES) % DIFF_HEAD_DIM) < (DIFF_HEAD_DIM // 2)
    sina = jnp.where(first_half[None, :], -sin, 0.0)
    sinb = jnp.where(first_half[None, :], 0.0, sin)
    return cos, sina, sinb


def _transpose_value_blocks(v_ref, vt_ref, tk):
    for c in range(vt_ref.shape[0]):
        vt_ref[c] = v_ref[c * tk:(c + 1) * tk, :].astype(F32).T.astype(BF16)


def _split_halves_t(q_ref):
    qt = q_ref[...].astype(F32).T
    sub = lax.broadcasted_iota(jnp.int32, qt.shape, 0)
    first = jnp.where(sub < LANES // 2, qt, 0.0).astype(BF16)
    second = jnp.where(sub >= LANES // 2, qt, 0.0).astype(BF16)
    return first, second, sub


def _diff_attn_kernel(lp_ref, g_ref, q_ref, k_ref, v_ref, o_ref, vt_ref, m_ref, l_ref, acc_ref, s_ref, *,
                      tq, tk, lambda_init):
    i = pl.program_id(2)

    @pl.when(i == 0)
    def _():
        _transpose_value_blocks(v_ref, vt_ref, tk)

    q_maps = _split_halves_t(q_ref)[:2]

    m_ref[...] = jnp.full(m_ref.shape, -jnp.inf, F32)
    l_ref[...] = jnp.zeros(l_ref.shape, F32)
    acc_ref[...] = jnp.zeros(acc_ref.shape, F32)

    def scores(j, mi):
        start = pl.multiple_of(j * tk, tk)
        return jnp.dot(k_ref[pl.ds(start, tk), :], q_maps[mi], preferred_element_type=F32)

    def block_softmax(s, j, masked):
        if masked:
            key = j * tk + lax.broadcasted_iota(jnp.int32, s.shape, 0)
            qry = i * tq + lax.broadcasted_iota(jnp.int32, s.shape, 1)
            s = jnp.where((key // CHUNK) <= (qry // CHUNK), s, -jnp.inf)
        m_blk = jnp.max(s, axis=0, keepdims=True)
        m_ref_pt = jnp.where(m_blk == -jnp.inf, 0.0, m_blk) if masked else m_blk
        p = jnp.exp2(s - m_ref_pt)
        l_blk = jnp.sum(p, axis=0, keepdims=True)
        pv = jnp.dot(vt_ref[j], p.astype(BF16), preferred_element_type=F32)
        return m_blk, l_blk, pv

    def combine(mi, blk):
        m_blk, l_blk, pv = blk
        m_prev = m_ref[mi]
        m_new = jnp.maximum(m_prev, m_blk)
        a_prev = jnp.exp2(m_prev - m_new)
        a_blk = jnp.exp2(m_blk - m_new)
        l_ref[mi] = a_prev * l_ref[mi] + a_blk * l_blk
        acc_ref[mi] = a_prev * acc_ref[mi] + a_blk * pv
        m_ref[mi] = m_new

    assert tq == 2 * tk
    chains = [(d, mi) for d in range(2) for mi in range(2)]

    for d, mi in chains:
        combine(mi, block_softmax(scores(2 * i + d, mi), 2 * i + d, True))

    @pl.when(i > 0)
    def _():
        for c, (d, mi) in enumerate(chains):
            s_ref[c] = scores(2 * (i - 1) + d, mi)

        def body(t, carry):
            j_cur = 2 * (i - 1 - t)
            for c, (d, mi) in enumerate(chains):
                blk = block_softmax(s_ref[c], j_cur + d, False)
                s_ref[c] = scores(j_cur - 2 + d, mi)
                combine(mi, blk)
            return carry

        lax.fori_loop(0, i - 1, body, 0)
        for c, (d, mi) in enumerate(chains):
            combine(mi, block_softmax(s_ref[c], d, False))

    lp = lp_ref[...]
    s01 = jnp.sum(lp[0:1, :] * lp[1:2, :], axis=1, keepdims=True)
    s23 = jnp.sum(lp[2:3, :] * lp[3:4, :], axis=1, keepdims=True)
    lam = jnp.exp(s01) - jnp.exp(s23) + lambda_init
    o = acc_ref[0] * (1.0 / l_ref[0]) - lam * (acc_ref[1] * (1.0 / l_ref[1]))
    o = o * lax.rsqrt(jnp.mean(o * o, axis=0, keepdims=True) + RMS_EPS)
    o = o * g_ref[...] * (1.0 - lambda_init)
    o_ref[...] = o.T.astype(o_ref.dtype)


def _diff_attention(qkv, lam_params, subln_g, *, batch, seq, lambda_init):
    T = qkv.shape[0]
    tq, tk = ATTN_TQ, ATTN_TK
    nq = seq // tq
    H = DIFF_HEADS
    return pl.pallas_call(
        functools.partial(_diff_attn_kernel, tq=tq, tk=tk, lambda_init=lambda_init),
        grid=(batch, H, nq),
        in_specs=[pl.BlockSpec((4, DIFF_HEAD_DIM), lambda b, h, i: (0, 0)),
                  pl.BlockSpec((LANES, 1), lambda b, h, i: (0, 0)),
                  pl.BlockSpec((tq, LANES), lambda b, h, i: (b * nq + i, h)),
                  pl.BlockSpec((seq, LANES), lambda b, h, i: (b, H + h)),
                  pl.BlockSpec((seq, LANES), lambda b, h, i: (b, 2 * H + h))],
        out_specs=pl.BlockSpec((tq, LANES), lambda b, h, i: (b * nq + i, h)),
        out_shape=jax.ShapeDtypeStruct((T, H * LANES), BF16),
        scratch_shapes=[pltpu.VMEM((seq // tk, LANES, tk), BF16),
                        pltpu.VMEM((2, 1, tq), F32), pltpu.VMEM((2, 1, tq), F32),
                        pltpu.VMEM((2, LANES, tq), F32), pltpu.VMEM((4, tk, tq), F32)],
        compiler_params=_cparams(("parallel", "parallel", "arbitrary")),
        name="diff_attn",
    )(lam_params, subln_g.reshape(LANES, 1), qkv, qkv, qkv)


def _stick_kernel(q_ref, k_ref, v_ref, o_ref, vt_ref, r_ref, acc_ref, z_ref, *, tq, tk):
    i = pl.program_id(2)

    @pl.when(i == 0)
    def _():
        _transpose_value_blocks(v_ref, vt_ref, tk)

    q_first, q_second, sub = _split_halves_t(q_ref)
    q_heads = (q_first, q_second)

    jj = lax.broadcasted_iota(jnp.int32, (tk + SUM_ROWS, 2 * tk), 0)
    kk = lax.broadcasted_iota(jnp.int32, (tk + SUM_ROWS, 2 * tk), 1) % tk
    suffix = jnp.where((kk >= jj) | (jj >= tk), 1.0, 0.0).astype(BF16)

    r_ref[...] = jnp.zeros(r_ref.shape, F32)
    acc_ref[...] = jnp.zeros(acc_ref.shape, F32)

    def scores(jb, hd):
        start = pl.multiple_of(jb * tk, tk)
        return jnp.dot(k_ref[pl.ds(start, tk), :], q_heads[hd], preferred_element_type=F32)

    def visible(jb, shape):
        key = jb * tk + lax.broadcasted_iota(jnp.int32, shape, 0)
        qry = i * tq + lax.broadcasted_iota(jnp.int32, shape, 1)
        return key < qry

    def split_logs(z, keep):
        nz = -z
        lp = jnp.log2(1.0 + jnp.exp2(jnp.minimum(z, nz)))
        lsm = jnp.minimum(nz, 0.0) - lp
        if keep is not None:
            lsm = jnp.where(keep, lsm, 0.0)
        hi = lsm.astype(BF16)
        return jnp.concatenate([hi, (lsm - hi.astype(F32)).astype(BF16)], axis=0)

    def suffix_sums(hi_lo):
        return jnp.dot(suffix, hi_lo, preferred_element_type=F32)

    def weighted_values(z, sums, jb, keep):
        a = jnp.exp2(z + sums[0:tk])
        if keep is not None:
            a = jnp.where(keep, a, 0.0)
        pv = jnp.dot(vt_ref[jb], a.astype(BF16), preferred_element_type=F32)
        return pv, sums[tk:tk + 1]

    def combine(hd, blk):
        pv, total = blk
        acc_ref[hd] = acc_ref[hd] + jnp.exp2(r_ref[hd]) * pv
        r_ref[hd] = r_ref[hd] + total

    assert tq == 2 * tk
    chains = [(d, hd) for d in range(2) for hd in range(2)]

    for d, hd in chains:
        jb = 2 * i + 1 - d
        z = scores(jb, hd)
        keep = visible(jb, z.shape)
        combine(hd, weighted_values(z, suffix_sums(split_logs(z, keep)), jb, keep))

    def pipelined_pair(jb_right, prefetch):
        hi_lo, sums, blk = {}, {}, {}

        def logs(c):
            hi_lo[c] = split_logs(z_ref[c], None)
            sums[c] = suffix_sums(hi_lo[c])

        def values(c):
            d, hd = chains[c]
            blk[c] = weighted_values(z_ref[c], sums[c], jb_right - d, None)
            if prefetch:
                z_ref[c] = scores(jb_right - 2 - d, hd)

        logs(0)
        logs(1)
        values(0)
        logs(2)
        values(1)
        logs(3)
        values(2)
        values(3)
        for c, (d, hd) in enumerate(chains):
            combine(hd, blk[c])

    @pl.when(i > 0)
    def _():
        for c, (d, hd) in enumerate(chains):
            z_ref[c] = scores(2 * i - 1 - d, hd)

        def body(t, carry):
            pipelined_pair(2 * i - 1 - 2 * t, True)
            return carry

        lax.fori_loop(0, i - 1, body, 0)
        pipelined_pair(1, False)

    out = jnp.where(sub < SB_HEAD_DIM, acc_ref[0], acc_ref[1])
    o_ref[...] = out.T.astype(o_ref.dtype)


def _stick_breaking(qkv, *, batch, seq):
    T = qkv.shape[0]
    tq, tk = SB_TQ, SB_TK
    nq = seq // tq
    P = SB_HEADS // 2
    return pl.pallas_call(
        functools.partial(_stick_kernel, tq=tq, tk=tk),
        grid=(batch, P, nq),
        in_specs=[pl.BlockSpec((tq, LANES), lambda b, p, i: (b * nq + i, p)),
                  pl.BlockSpec((seq, LANES), lambda b, p, i: (b, P + p)),
                  pl.BlockSpec((seq, LANES), lambda b, p, i: (b, 2 * P + p))],
        out_specs=pl.BlockSpec((tq, LANES), lambda b, p, i: (b * nq + i, p)),
        out_shape=jax.ShapeDtypeStruct((T, P * LANES), BF16),
        scratch_shapes=[pltpu.VMEM((seq // tk, LANES, tk), BF16),
                        pltpu.VMEM((2, 1, tq), F32), pltpu.VMEM((2, LANES, tq), F32),
                        pltpu.VMEM((4, tk, tq), F32)],
        compiler_params=_cparams(("parallel", "parallel", "arbitrary")),
        name="stick_breaking",
    )(qkv, qkv, qkv)


def _layer_norm_rows(y, g, b):
    mu = jnp.mean(y, axis=1, keepdims=True)
    d = y - mu
    var = jnp.mean(d * d, axis=1, keepdims=True)
    return d * lax.rsqrt(var + LN_EPS) * g + b


def _proj_ln_kernel(a_ref, w_ref, x_ref, g_ref, b_ref, o_ref):
    y = jnp.dot(a_ref[...], w_ref[...], preferred_element_type=F32) + DEEPNORM_ALPHA * x_ref[...]
    o_ref[...] = _layer_norm_rows(y, g_ref[...], b_ref[...])


def _proj_ln(a_bf16, w_bf16, x2d, g, b):
    T, D = x2d.shape
    tm = PROJ_TM
    return pl.pallas_call(
        _proj_ln_kernel,
        grid=(T // tm,),
        in_specs=[pl.BlockSpec((tm, D), lambda i: (i, 0)),
                  pl.BlockSpec((D, D), lambda i: (0, 0)),
                  pl.BlockSpec((tm, D), lambda i: (i, 0)),
                  pl.BlockSpec((1, D), lambda i: (0, 0)),
                  pl.BlockSpec((1, D), lambda i: (0, 0))],
        out_specs=pl.BlockSpec((tm, D), lambda i: (i, 0)),
        out_shape=jax.ShapeDtypeStruct((T, D), F32),
        compiler_params=_cparams(("parallel",)),
        name="proj_ln",
    )(a_bf16, w_bf16, x2d, g.reshape(1, D), b.reshape(1, D))


N_TOP = PEER_TOPK + 1


def _extract_top(scores, n, with_rank):
    work = scores
    vals = []
    rank = jnp.full(scores.shape, float(PEER_N_KEYS), F32) if with_rank else None
    for r in range(n):
        m = jnp.max(work, axis=0, keepdims=True)
        vals.append(m)
        hit = work == m
        if with_rank:
            rank = jnp.where(hit, float(r), rank)
        work = jnp.where(hit, -jnp.inf, work)
    return vals, rank


def _route_kernel(x_ref, w_ref, kp_ref, rb_ref, e2_ref, n_ref, a1_ref):
    x = x_ref[...].astype(BF16)
    qp = jnp.dot(x, w_ref[...], preferred_element_type=F32).astype(BF16)
    tm = x.shape[0]
    for h in range(PEER_HEADS):
        qh = qp[:, h * LANES:(h + 1) * LANES]
        nt = (((1,), (1,)), ((), ()))
        s1 = lax.dot_general(kp_ref[h, 0], qh, nt, preferred_element_type=F32)
        s2 = lax.dot_general(kp_ref[h, 1], qh, nt, preferred_element_type=F32)
        top1, _ = _extract_top(s1, N_TOP, False)
        top2, rank2 = _extract_top(s2, N_TOP, True)
        cands = [top1[i] + top2[j] for i in range(N_TOP) for j in range(N_TOP)
                 if (i + 1) * (j + 1) <= N_TOP]
        pad = (-len(cands)) % 8
        cands += [jnp.full((1, tm), -jnp.inf, F32)] * pad
        ctop, _ = _extract_top(jnp.concatenate(cands, axis=0), N_TOP, False)
        cut = 0.5 * (ctop[PEER_TOPK - 1] + ctop[PEER_TOPK])
        cmax = ctop[0]
        z = ctop[0] - cmax
        z = jnp.exp(z)
        for r in range(1, PEER_TOPK):
            z = z + jnp.exp(ctop[r] - cmax)
        count1 = jnp.zeros(s1.shape, F32)
        for j in range(PEER_TOPK):
            count1 = count1 + jnp.where(s1 + top2[j] >= cut, 1.0, 0.0)
        rb_ref[h] = rank2
        e2_ref[h] = jnp.exp(s2 - top2[0])
        n_ref[h] = count1
        a1_ref[h] = jnp.exp(s1 - top1[0]) / z


def _peer_route(x2d, w_pq_bf16, keys_padded):
    T, D = x2d.shape
    tm = ROUTE_TM
    out_spec = pl.BlockSpec((PEER_HEADS, PEER_N_KEYS, tm), lambda i: (0, 0, i))
    out_sds = jax.ShapeDtypeStruct((PEER_HEADS, PEER_N_KEYS, T), F32)
    return pl.pallas_call(
        _route_kernel,
        grid=(T // tm,),
        in_specs=[pl.BlockSpec((tm, D), lambda i: (i, 0)),
                  pl.BlockSpec((D, PEER_HEADS * LANES), lambda i: (0, 0)),
                  pl.BlockSpec((PEER_HEADS, 2, PEER_N_KEYS, LANES), lambda i: (0, 0, 0, 0))],
        out_specs=[out_spec] * 4,
        out_shape=[out_sds] * 4,
        compiler_params=_cparams(("parallel",)),
        name="peer_route",
    )(x2d, w_pq_bf16, keys_padded)


def _peer_dense_kernel(x_ref, u_ref, vt_ref, rb_ref, e2_ref, n_ref, a1_ref, g_ref, b_ref, o_ref,
                       xt_ref, acc_ref, gs_ref, *, te):
    e = pl.program_id(1)

    @pl.when(e == 0)
    def _():
        xt_ref[...] = x_ref[...].T.astype(BF16)
        acc_ref[...] = jnp.zeros(acc_ref.shape, F32)

    ht = jnp.dot(u_ref[...], xt_ref[...], preferred_element_type=F32)
    for i in range(te // PEER_N_KEYS):
        hi = ht[i * PEER_N_KEYS:(i + 1) * PEER_N_KEYS, :]
        w = jnp.zeros(hi.shape, F32)
        for h in range(PEER_HEADS):
            cnt = n_ref[h, i:i + 1, :]
            a1 = a1_ref[h, i:i + 1, :]
            w = w + a1 * jnp.where(rb_ref[h] < cnt, e2_ref[h], 0.0)
        gelu = 0.5 * hi * (1.0 + lax.erf(hi * (1.0 / math.sqrt(2.0))))
        gs_ref[i * PEER_N_KEYS:(i + 1) * PEER_N_KEYS, :] = (w * gelu).astype(BF16)
    acc_ref[...] += jnp.dot(vt_ref[...], gs_ref[...], preferred_element_type=F32)

    @pl.when(e == pl.num_programs(1) - 1)
    def _():
        y = acc_ref[...].T + DEEPNORM_ALPHA * x_ref[...]
        o_ref[...] = _layer_norm_rows(y, g_ref[...], b_ref[...])


def _peer_dense(x2d, u_bf16, vt_bf16, routing, g, b):
    T, D = x2d.shape
    E = u_bf16.shape[0]
    tm, te = DENSE_TM, DENSE_TE
    rb, e2, cnt, a1 = routing
    full_spec = pl.BlockSpec((PEER_HEADS, PEER_N_KEYS, tm), lambda t, e: (0, 0, t))
    row_spec = pl.BlockSpec((PEER_HEADS, te // PEER_N_KEYS, tm), lambda t, e: (0, e, t))
    return pl.pallas_call(
        functools.partial(_peer_dense_kernel, te=te),
        grid=(T // tm, E // te),
        in_specs=[pl.BlockSpec((tm, D), lambda t, e: (t, 0)),
                  pl.BlockSpec((te, D), lambda t, e: (e, 0)),
                  pl.BlockSpec((D, te), lambda t, e: (0, e)),
                  full_spec, full_spec, row_spec, row_spec,
                  pl.BlockSpec((1, D), lambda t, e: (0, 0)),
                  pl.BlockSpec((1, D), lambda t, e: (0, 0))],
        out_specs=pl.BlockSpec((tm, D), lambda t, e: (t, 0)),
        out_shape=jax.ShapeDtypeStruct((T, D), F32),
        scratch_shapes=[pltpu.VMEM((D, tm), BF16), pltpu.VMEM((D, tm), F32), pltpu.VMEM((te, tm), BF16)],
        compiler_params=_cparams(("parallel", "arbitrary")),
        name="peer_dense",
    )(x2d, u_bf16, vt_bf16, rb, e2, cnt, a1, g.reshape(1, D), b.reshape(1, D))


def _pad_sub_keys(sub_keys):
    z = jnp.zeros_like(sub_keys[:, 0])
    first = jnp.concatenate([sub_keys[:, 0], z], axis=-1)
    second = jnp.concatenate([z, sub_keys[:, 1]], axis=-1)
    return jnp.stack([first, second], axis=1).astype(BF16)


def _peer(x2d, w_pq, sub_keys, u, v, g, b):
    routing = _peer_route(x2d, w_pq.astype(BF16), _pad_sub_keys(sub_keys))
    return _peer_dense(x2d, u.astype(BF16), v.T.astype(BF16), routing, g, b)


def kernel(x, ln_g, ln_b, w_qkv_a, w_o_a, lambda_qk_a, subln_g_a, w_kv_b, w_q_b, w_o_b,
           peer_w_q, peer_sub_keys, peer_u, peer_v):
    B, S, D = x.shape
    x2d = x.reshape(B * S, D)
    tables = _rope_tables(S)
    n_qk_groups = 2 * D // LANES

    lambda_init = 0.8 - 0.6 * math.exp(-0.3 * 0)
    qkv = _proj(x2d, w_qkv_a[0].astype(BF16), tables, seq=S, rope_groups=n_qk_groups,
                scale_groups=D // LANES, scale=DIFF_HEAD_DIM ** -0.5 * LOG2E)
    att = _diff_attention(qkv, lambda_qk_a[0], subln_g_a[0], batch=B, seq=S, lambda_init=lambda_init)
    x2d = _proj_ln(att, w_o_a[0].astype(BF16), x2d, ln_g[0, 0], ln_b[0, 0])
    x2d = _peer(x2d, peer_w_q[0], peer_sub_keys[0], peer_u[0], peer_v[0], ln_g[0, 1], ln_b[0, 1])

    w_qkv_b = jnp.concatenate([w_q_b[0], w_kv_b], axis=1).astype(BF16)
    qkv = _proj(x2d, w_qkv_b, tables, seq=S, rope_groups=0,
                scale_groups=D // LANES, scale=SB_HEAD_DIM ** -0.5 * LOG2E)
    sb = _stick_breaking(qkv, batch=B, seq=S)
    x2d = _proj_ln(sb, w_o_b[0].astype(BF16), x2d, ln_g[1, 0], ln_b[1, 0])
    x2d = _peer(x2d, peer_w_q[1], peer_sub_keys[1], peer_u[1], peer_v[1], ln_g[1, 1], ln_b[1, 1])
    return x2d.reshape(B, S, D)
```

```python
import functools
import math

import jax
import jax.numpy as jnp
from jax import lax
from jax.experimental import pallas as pl
from jax.experimental.pallas import tpu as pltpu

F32 = jnp.float32
BF16 = jnp.bfloat16

DIFF_HEADS = 8
DIFF_HEAD_DIM = 64
SB_HEADS = 16
SB_HEAD_DIM = 64
PEER_HEADS = 8
PEER_N_KEYS = 128
PEER_TOPK = 16
PEER_HALF = 64
CHUNK = 64
ROPE_THETA = 10000.0
LN_EPS = 1e-5
RMS_EPS = 1e-5
DEPTH = 2
DEEPNORM_ALPHA = (2.0 * DEPTH) ** 0.25

LANES = 128
SUM_ROWS = 16
LOG2E = math.log2(math.e)

PROJ_TM = 256
ATTN_TQ = 512
ATTN_TK = 256
SB_TQ = 512
SB_TK = 256
ROUTE_TM = 256
DENSE_TM = 512
DENSE_TE = 1024
VMEM_LIMIT = 56 * 1024 * 1024


def _cparams(sem):
    return pltpu.CompilerParams(dimension_semantics=sem, vmem_limit_bytes=VMEM_LIMIT)


def _proj_kernel(x_ref, w_ref, cos_ref, sina_ref, sinb_ref, o_ref, *, rope_groups, scale_groups, scale):
    x = x_ref[...].astype(BF16)
    acc = jnp.dot(x, w_ref[...], preferred_element_type=F32)
    n_groups = acc.shape[1] // LANES
    for g in range(n_groups):
        t = acc[:, g * LANES:(g + 1) * LANES]
        if g < rope_groups:
            t = (t * cos_ref[...] + pltpu.roll(t, LANES - 32, 1) * sina_ref[...]
                 + pltpu.roll(t, 32, 1) * sinb_ref[...])
        if g < scale_groups:
            t = t * scale
        o_ref[:, g * LANES:(g + 1) * LANES] = t.astype(o_ref.dtype)


def _proj(x2d, w_bf16, tables, *, seq, rope_groups, scale_groups, scale):
    T, D = x2d.shape
    N = w_bf16.shape[1]
    tm = PROJ_TM
    n_seq_blocks = seq // tm
    cos, sina, sinb = tables
    tab_spec = pl.BlockSpec((tm, LANES), lambda i: (i % n_seq_blocks, 0))
    return pl.pallas_call(
        functools.partial(_proj_kernel, rope_groups=rope_groups, scale_groups=scale_groups, scale=scale),
        grid=(T // tm,),
        in_specs=[pl.BlockSpec((tm, D), lambda i: (i, 0)),
                  pl.BlockSpec((D, N), lambda i: (0, 0)),
                  tab_spec, tab_spec, tab_spec],
        out_specs=pl.BlockSpec((tm, N), lambda i: (i, 0)),
        out_shape=jax.ShapeDtypeStruct((T, N), BF16),
        compiler_params=_cparams(("parallel",)),
        name="proj",
    )(x2d, w_bf16, cos, sina, sinb)


def _rope_tables(seq):
    pos = jnp.arange(seq, dtype=F32)
    inv = ROPE_THETA ** (-jnp.arange(0, DIFF_HEAD_DIM, 2, dtype=F32) / DIFF_HEAD_DIM)
    ang = pos[:, None] * inv[None, :]
    ang = jnp.concatenate([ang, ang, ang, ang], axis=-1)
    cos, sin = jnp.cos(ang), jnp.sin(ang)
    first_half = (jnp.arange(LANES) % DIFF_HEAD_DIM) < (DIFF_HEAD_DIM // 2)
    sina = jnp.where(first_half[None, :], -sin, 0.0)
    sinb = jnp.where(first_half[None, :], 0.0, sin)
    return cos, sina, sinb


def _transpose_value_blocks(v_ref, vt_ref, tk):
    for c in range(vt_ref.shape[0]):
        vt_ref[c] = v_ref[c * tk:(c + 1) * tk, :].astype(F32).T.astype(BF16)


def _split_halves_t(q_ref):
    qt = q_ref[...].astype(F32).T
    sub = lax.broadcasted_iota(jnp.int32, qt.shape, 0)
    first = jnp.where(sub < LANES // 2, qt, 0.0).astype(BF16)
    second = jnp.where(sub >= LANES // 2, qt, 0.0).astype(BF16)
    return first, second, sub


def _diff_attn_kernel(lp_ref, g_ref, q_ref, k_ref, v_ref, o_ref, vt_ref, m_ref, l_ref, acc_ref, s_ref, *,
                      tq, tk, lambda_init):
    i = pl.program_id(2)

    @pl.when(i == 0)
    def _():
        _transpose_value_blocks(v_ref, vt_ref, tk)

    q_maps = _split_halves_t(q_ref)[:2]

    m_ref[...] = jnp.full(m_ref.shape, -jnp.inf, F32)
    l_ref[...] = jnp.zeros(l_ref.shape, F32)
    acc_ref[...] = jnp.zeros(acc_ref.shape, F32)

    def scores(j, mi):
        start = pl.multiple_of(j * tk, tk)
        return jnp.dot(k_ref[pl.ds(start, tk), :], q_maps[mi], preferred_element_type=F32)

    def block_softmax(s, j, masked):
        if masked:
            key = j * tk + lax.broadcasted_iota(jnp.int32, s.shape, 0)
            qry = i * tq + lax.broadcasted_iota(jnp.int32, s.shape, 1)
            s = jnp.where((key // CHUNK) <= (qry // CHUNK), s, -jnp.inf)
        m_blk = jnp.max(s, axis=0, keepdims=True)
        m_ref_pt = jnp.where(m_blk == -jnp.inf, 0.0, m_blk) if masked else m_blk
        p = jnp.exp2(s - m_ref_pt)
        l_blk = jnp.sum(p, axis=0, keepdims=True)
        pv = jnp.dot(vt_ref[j], p.astype(BF16), preferred_element_type=F32)
        return m_blk, l_blk, pv

    def combine(mi, blk):
        m_blk, l_blk, pv = blk
        m_prev = m_ref[mi]
        m_new = jnp.maximum(m_prev, m_blk)
        a_prev = jnp.exp2(m_prev - m_new)
        a_blk = jnp.exp2(m_blk - m_new)
        l_ref[mi] = a_prev * l_ref[mi] + a_blk * l_blk
        acc_ref[mi] = a_prev * acc_ref[mi] + a_blk * pv
        m_ref[mi] = m_new

    assert tq == 2 * tk
    chains = [(d, mi) for d in range(2) for mi in range(2)]

    for d, mi in chains:
        combine(mi, block_softmax(scores(2 * i + d, mi), 2 * i + d, True))

    @pl.when(i > 0)
    def _():
        for c, (d, mi) in enumerate(chains):
            s_ref[c] = scores(2 * (i - 1) + d, mi)

        def body(t, carry):
            j_cur = 2 * (i - 1 - t)
            for c, (d, mi) in enumerate(chains):
                blk = block_softmax(s_ref[c], j_cur + d, False)
                s_ref[c] = scores(j_cur - 2 + d, mi)
                combine(mi, blk)
            return carry

        lax.fori_loop(0, i - 1, body, 0)
        for c, (d, mi) in enumerate(chains):
            combine(mi, block_softmax(s_ref[c], d, False))

    lp = lp_ref[...]
    s01 = jnp.sum(lp[0:1, :] * lp[1:2, :], axis=1, keepdims=True)
    s23 = jnp.sum(lp[2:3, :] * lp[3:4, :], axis=1, keepdims=True)
    lam = jnp.exp(s01) - jnp.exp(s23) + lambda_init
    o = acc_ref[0] * (1.0 / l_ref[0]) - lam * (acc_ref[1] * (1.0 / l_ref[1]))
    o = o * lax.rsqrt(jnp.mean(o * o, axis=0, keepdims=True) + RMS_EPS)
    o = o * g_ref[...] * (1.0 - lambda_init)
    o_ref[...] = o.T.astype(o_ref.dtype)


def _diff_attention(qkv, lam_params, subln_g, *, batch, seq, lambda_init):
    T = qkv.shape[0]
    tq, tk = ATTN_TQ, ATTN_TK
    nq = seq // tq
    H = DIFF_HEADS
    return pl.pallas_call(
        functools.partial(_diff_attn_kernel, tq=tq, tk=tk, lambda_init=lambda_init),
        grid=(batch, H, nq),
        in_specs=[pl.BlockSpec((4, DIFF_HEAD_DIM), lambda b, h, i: (0, 0)),
                  pl.BlockSpec((LANES, 1), lambda b, h, i: (0, 0)),
                  pl.BlockSpec((tq, LANES), lambda b, h, i: (b * nq + i, h)),
                  pl.BlockSpec((seq, LANES), lambda b, h, i: (b, H + h)),
                  pl.BlockSpec((seq, LANES), lambda b, h, i: (b, 2 * H + h))],
        out_specs=pl.BlockSpec((tq, LANES), lambda b, h, i: (b * nq + i, h)),
        out_shape=jax.ShapeDtypeStruct((T, H * LANES), BF16),
        scratch_shapes=[pltpu.VMEM((seq // tk, LANES, tk), BF16),
                        pltpu.VMEM((2, 1, tq), F32), pltpu.VMEM((2, 1, tq), F32),
                        pltpu.VMEM((2, LANES, tq), F32), pltpu.VMEM((4, tk, tq), F32)],
        compiler_params=_cparams(("parallel", "parallel", "arbitrary")),
        name="diff_attn",
    )(lam_params, subln_g.reshape(LANES, 1), qkv, qkv, qkv)


def _stick_kernel(q_ref, k_ref, v_ref, o_ref, vt_ref, r_ref, acc_ref, z_ref, *, tq, tk):
    i = pl.program_id(2)

    @pl.when(i == 0)
    def _():
        _transpose_value_blocks(v_ref, vt_ref, tk)

    q_first, q_second, sub = _split_halves_t(q_ref)
    q_heads = (q_first, q_second)

    jj = lax.broadcasted_iota(jnp.int32, (tk + SUM_ROWS, 2 * tk), 0)
    kk = lax.broadcasted_iota(jnp.int32, (tk + SUM_ROWS, 2 * tk), 1) % tk
    suffix = jnp.where((kk >= jj) | (jj >= tk), 1.0, 0.0).astype(BF16)

    r_ref[...] = jnp.zeros(r_ref.shape, F32)
    acc_ref[...] = jnp.zeros(acc_ref.shape, F32)

    def scores(jb, hd):
        start = pl.multiple_of(jb * tk, tk)
        return jnp.dot(k_ref[pl.ds(start, tk), :], q_heads[hd], preferred_element_type=F32)

    def visible(jb, shape):
        key = jb * tk + lax.broadcasted_iota(jnp.int32, shape, 0)
        qry = i * tq + lax.broadcasted_iota(jnp.int32, shape, 1)
        return key < qry

    def split_logs(z, keep):
        nz = -z
        lp = jnp.log2(1.0 + jnp.exp2(jnp.minimum(z, nz)))
        lsm = jnp.minimum(nz, 0.0) - lp
        if keep is not None:
            lsm = jnp.where(keep, lsm, 0.0)
        hi = lsm.astype(BF16)
        return jnp.concatenate([hi, (lsm - hi.astype(F32)).astype(BF16)], axis=0)

    def suffix_sums(hi_lo):
        return jnp.dot(suffix, hi_lo, preferred_element_type=F32)

    def weighted_values(z, sums, jb, keep):
        a = jnp.exp2(z + sums[0:tk])
        if keep is not None:
            a = jnp.where(keep, a, 0.0)
        pv = jnp.dot(vt_ref[jb], a.astype(BF16), preferred_element_type=F32)
        return pv, sums[tk:tk + 1]

    def combine(hd, blk):
        pv, total = blk
        acc_ref[hd] = acc_ref[hd] + jnp.exp2(r_ref[hd]) * pv
        r_ref[hd] = r_ref[hd] + total

    assert tq == 2 * tk
    chains = [(d, hd) for d in range(2) for hd in range(2)]

    for d, hd in chains:
        jb = 2 * i + 1 - d
        z = scores(jb, hd)
        keep = visible(jb, z.shape)
        combine(hd, weighted_values(z, suffix_sums(split_logs(z, keep)), jb, keep))

    def pipelined_pair(jb_right, prefetch):
        hi_lo, sums, blk = {}, {}, {}

        def logs(c):
            hi_lo[c] = split_logs(z_ref[c], None)
            sums[c] = suffix_sums(hi_lo[c])

        def values(c):
            d, hd = chains[c]
            blk[c] = weighted_values(z_ref[c], sums[c], jb_right - d, None)
            if prefetch:
                z_ref[c] = scores(jb_right - 2 - d, hd)

        logs(0)
        logs(1)
        values(0)
        logs(2)
        values(1)
        logs(3)
        values(2)
        values(3)
        for c, (d, hd) in enumerate(chains):
            combine(hd, blk[c])

    @pl.when(i > 0)
    def _():
        for c, (d, hd) in enumerate(chains):
            z_ref[c] = scores(2 * i - 1 - d, hd)

        def body(t, carry):
            pipelined_pair(2 * i - 1 - 2 * t, True)
            return carry

        lax.fori_loop(0, i - 1, body, 0)
        pipelined_pair(1, False)

    out = jnp.where(sub < SB_HEAD_DIM, acc_ref[0], acc_ref[1])
    o_ref[...] = out.T.astype(o_ref.dtype)


def _stick_breaking(qkv, *, batch, seq):
    T = qkv.shape[0]
    tq, tk = SB_TQ, SB_TK
    nq = seq // tq
    P = SB_HEADS // 2
    return pl.pallas_call(
        functools.partial(_stick_kernel, tq=tq, tk=tk),
        grid=(batch, P, nq),
        in_specs=[pl.BlockSpec((tq, LANES), lambda b, p, i: (b * nq + i, p)),
                  pl.BlockSpec((seq, LANES), lambda b, p, i: (b, P + p)),
                  pl.BlockSpec((seq, LANES), lambda b, p, i: (b, 2 * P + p))],
        out_specs=pl.BlockSpec((tq, LANES), lambda b, p, i: (b * nq + i, p)),
        out_shape=jax.ShapeDtypeStruct((T, P * LANES), BF16),
        scratch_shapes=[pltpu.VMEM((seq // tk, LANES, tk), BF16),
                        pltpu.VMEM((2, 1, tq), F32), pltpu.VMEM((2, LANES, tq), F32),
                        pltpu.VMEM((4, tk, tq), F32)],
        compiler_params=_cparams(("parallel", "parallel", "arbitrary")),
        name="stick_breaking",
    )(qkv, qkv, qkv)


def _layer_norm_rows(y, g, b):
    mu = jnp.mean(y, axis=1, keepdims=True)
    d = y - mu
    var = jnp.mean(d * d, axis=1, keepdims=True)
    return d * lax.rsqrt(var + LN_EPS) * g + b


def _proj_ln_kernel(a_ref, w_ref, x_ref, g_ref, b_ref, o_ref):
    y = jnp.dot(a_ref[...], w_ref[...], preferred_element_type=F32) + DEEPNORM_ALPHA * x_ref[...]
    o_ref[...] = _layer_norm_rows(y, g_ref[...], b_ref[...])


def _proj_ln(a_bf16, w_bf16, x2d, g, b):
    T, D = x2d.shape
    tm = PROJ_TM
    return pl.pallas_call(
        _proj_ln_kernel,
        grid=(T // tm,),
        in_specs=[pl.BlockSpec((tm, D), lambda i: (i, 0)),
                  pl.BlockSpec((D, D), lambda i: (0, 0)),
                  pl.BlockSpec((tm, D), lambda i: (i, 0)),
                  pl.BlockSpec((1, D), lambda i: (0, 0)),
                  pl.BlockSpec((1, D), lambda i: (0, 0))],
        out_specs=pl.BlockSpec((tm, D), lambda i: (i, 0)),
        out_shape=jax.ShapeDtypeStruct((T, D), F32),
        compiler_params=_cparams(("parallel",)),
        name="proj_ln",
    )(a_bf16, w_bf16, x2d, g.reshape(1, D), b.reshape(1, D))


N_TOP = PEER_TOPK + 1


def _extract_top(scores, n, with_rank):
    work = scores
    vals = []
    rank = jnp.full(scores.shape, float(PEER_N_KEYS), F32) if with_rank else None
    for r in range(n):
        m = jnp.max(work, axis=0, keepdims=True)
        vals.append(m)
        hit = work == m
        if with_rank:
            rank = jnp.where(hit, float(r), rank)
        work = jnp.where(hit, -jnp.inf, work)
    return vals, rank


def _route_kernel(x_ref, w_ref, kp_ref, rb_ref, e2_ref, n_ref, a1_ref):
    x = x_ref[...].astype(BF16)
    qp = jnp.dot(x, w_ref[...], preferred_element_type=F32).astype(BF16)
    tm = x.shape[0]
    for h in range(PEER_HEADS):
        qh = qp[:, h * LANES:(h + 1) * LANES]
        nt = (((1,), (1,)), ((), ()))
        s1 = lax.dot_general(kp_ref[h, 0], qh, nt, preferred_element_type=F32)
        s2 = lax.dot_general(kp_ref[h, 1], qh, nt, preferred_element_type=F32)
        top1, _ = _extract_top(s1, N_TOP, False)
        top2, rank2 = _extract_top(s2, N_TOP, True)
        cands = [top1[i] + top2[j] for i in range(N_TOP) for j in range(N_TOP)
                 if (i + 1) * (j + 1) <= N_TOP]
        pad = (-len(cands)) % 8
        cands += [jnp.full((1, tm), -jnp.inf, F32)] * pad
        ctop, _ = _extract_top(jnp.concatenate(cands, axis=0), N_TOP, False)
        cut = 0.5 * (ctop[PEER_TOPK - 1] + ctop[PEER_TOPK])
        cmax = ctop[0]
        z = ctop[0] - cmax
        z = jnp.exp(z)
        for r in range(1, PEER_TOPK):
            z = z + jnp.exp(ctop[r] - cmax)
        count1 = jnp.zeros(s1.shape, F32)
        for j in range(PEER_TOPK):
            count1 = count1 + jnp.where(s1 + top2[j] >= cut, 1.0, 0.0)
        rb_ref[h] = rank2.astype(BF16)
        e2_ref[h] = jnp.exp(s2 - top2[0]).astype(BF16)
        n_ref[h] = count1
        a1_ref[h] = jnp.exp(s1 - top1[0]) / z


def _peer_route(x2d, w_pq_bf16, keys_padded):
    T, D = x2d.shape
    tm = ROUTE_TM
    out_spec = pl.BlockSpec((PEER_HEADS, PEER_N_KEYS, tm), lambda i: (0, 0, i))
    sds = lambda dtype: jax.ShapeDtypeStruct((PEER_HEADS, PEER_N_KEYS, T), dtype)
    return pl.pallas_call(
        _route_kernel,
        grid=(T // tm,),
        in_specs=[pl.BlockSpec((tm, D), lambda i: (i, 0)),
                  pl.BlockSpec((D, PEER_HEADS * LANES), lambda i: (0, 0)),
                  pl.BlockSpec((PEER_HEADS, 2, PEER_N_KEYS, LANES), lambda i: (0, 0, 0, 0))],
        out_specs=[out_spec] * 4,
        out_shape=[sds(BF16), sds(BF16), sds(F32), sds(F32)],
        compiler_params=_cparams(("parallel",)),
        name="peer_route",
    )(x2d, w_pq_bf16, keys_padded)


def _peer_dense_kernel(x_ref, u_ref, vt_ref, rb_ref, e2_ref, n_ref, a1_ref, g_ref, b_ref, o_ref,
                       xt_ref, acc_ref, ht_ref, gs_ref, *, te):
    e = pl.program_id(1)
    n_tiles = pl.num_programs(1) - 1

    @pl.when(e == 0)
    def _():
        xt_ref[...] = x_ref[...].T.astype(BF16)
        acc_ref[...] = jnp.zeros(acc_ref.shape, F32)
        ht_ref[1] = jnp.zeros(ht_ref.shape[1:], F32)

    ht_ref[e % 2] = jnp.dot(u_ref[...], xt_ref[...], preferred_element_type=F32)

    ht_prev = ht_ref.at[(e + 1) % 2]
    for i in range(te // PEER_N_KEYS):
        rows = slice(i * PEER_N_KEYS, (i + 1) * PEER_N_KEYS)
        w = None
        for h in range(PEER_HEADS):
            cnt = n_ref[h, i:i + 1, :].astype(BF16)
            a1 = a1_ref[h, i:i + 1, :].astype(BF16)
            term = a1 * jnp.where(rb_ref[h] < cnt, e2_ref[h], 0)
            w = term if w is None else w + term
        hb = ht_prev[rows, :].astype(BF16)
        gelu = (0.5 * hb) * (1.0 + lax.erf(hb * (1.0 / math.sqrt(2.0))))
        gs_ref[rows, :] = w * gelu
    acc_ref[...] += jnp.dot(vt_ref[...], gs_ref[...], preferred_element_type=F32)

    @pl.when(e == n_tiles)
    def _():
        y = acc_ref[...].T + DEEPNORM_ALPHA * x_ref[...]
        o_ref[...] = _layer_norm_rows(y, g_ref[...], b_ref[...])


def _peer_dense(x2d, u_bf16, vt_bf16, routing, g, b):
    T, D = x2d.shape
    E = u_bf16.shape[0]
    tm, te = DENSE_TM, DENSE_TE
    n_tiles = E // te
    rb, e2, cnt, a1 = routing
    prev = lambda e: jnp.maximum(e - 1, 0)
    full_spec = pl.BlockSpec((PEER_HEADS, PEER_N_KEYS, tm), lambda t, e: (0, 0, t))
    row_spec = pl.BlockSpec((PEER_HEADS, te // PEER_N_KEYS, tm), lambda t, e: (0, prev(e), t))
    return pl.pallas_call(
        functools.partial(_peer_dense_kernel, te=te),
        grid=(T // tm, n_tiles + 1),
        in_specs=[pl.BlockSpec((tm, D), lambda t, e: (t, 0)),
                  pl.BlockSpec((te, D), lambda t, e: (jnp.minimum(e, n_tiles - 1), 0)),
                  pl.BlockSpec((D, te), lambda t, e: (0, prev(e))),
                  full_spec, full_spec, row_spec, row_spec,
                  pl.BlockSpec((1, D), lambda t, e: (0, 0)),
                  pl.BlockSpec((1, D), lambda t, e: (0, 0))],
        out_specs=pl.BlockSpec((tm, D), lambda t, e: (t, 0)),
        out_shape=jax.ShapeDtypeStruct((T, D), F32),
        scratch_shapes=[pltpu.VMEM((D, tm), BF16), pltpu.VMEM((D, tm), F32),
                        pltpu.VMEM((2, te, tm), F32), pltpu.VMEM((te, tm), BF16)],
        compiler_params=_cparams(("parallel", "arbitrary")),
        name="peer_dense",
    )(x2d, u_bf16, vt_bf16, rb, e2, cnt, a1, g.reshape(1, D), b.reshape(1, D))


def _pad_sub_keys(sub_keys):
    z = jnp.zeros_like(sub_keys[:, 0])
    first = jnp.concatenate([sub_keys[:, 0], z], axis=-1)
    second = jnp.concatenate([z, sub_keys[:, 1]], axis=-1)
    return jnp.stack([first, second], axis=1).astype(BF16)


def _peer(x2d, w_pq, sub_keys, u, v, g, b):
    routing = _peer_route(x2d, w_pq.astype(BF16), _pad_sub_keys(sub_keys))
    return _peer_dense(x2d, u.astype(BF16), v.T.astype(BF16), routing, g, b)


def kernel(x, ln_g, ln_b, w_qkv_a, w_o_a, lambda_qk_a, subln_g_a, w_kv_b, w_q_b, w_o_b,
           peer_w_q, peer_sub_keys, peer_u, peer_v):
    B, S, D = x.shape
    x2d = x.reshape(B * S, D)
    tables = _rope_tables(S)
    n_qk_groups = 2 * D // LANES

    lambda_init = 0.8 - 0.6 * math.exp(-0.3 * 0)
    qkv = _proj(x2d, w_qkv_a[0].astype(BF16), tables, seq=S, rope_groups=n_qk_groups,
                scale_groups=D // LANES, scale=DIFF_HEAD_DIM ** -0.5 * LOG2E)
    att = _diff_attention(qkv, lambda_qk_a[0], subln_g_a[0], batch=B, seq=S, lambda_init=lambda_init)
    x2d = _proj_ln(att, w_o_a[0].astype(BF16), x2d, ln_g[0, 0], ln_b[0, 0])
    x2d = _peer(x2d, peer_w_q[0], peer_sub_keys[0], peer_u[0], peer_v[0], ln_g[0, 1], ln_b[0, 1])

    w_qkv_b = jnp.concatenate([w_q_b[0], w_kv_b], axis=1).astype(BF16)
    qkv = _proj(x2d, w_qkv_b, tables, seq=S, rope_groups=0,
                scale_groups=D // LANES, scale=SB_HEAD_DIM ** -0.5 * LOG2E)
    sb = _stick_breaking(qkv, batch=B, seq=S)
    x2d = _proj_ln(sb, w_o_b[0].astype(BF16), x2d, ln_g[1, 0], ln_b[1, 0])
    x2d = _peer(x2d, peer_w_q[1], peer_sub_keys[1], peer_u[1], peer_v[1], ln_g[1, 1], ln_b[1, 1])
    return x2d.reshape(B, S, D)
```

```python
import functools
import math

import jax
import jax.numpy as jnp
from jax import lax
from jax.experimental import pallas as pl
from jax.experimental.pallas import tpu as pltpu

F32 = jnp.float32
BF16 = jnp.bfloat16

DIFF_HEADS = 8
DIFF_HEAD_DIM = 64
SB_HEADS = 16
SB_HEAD_DIM = 64
PEER_HEADS = 8
PEER_N_KEYS = 128
PEER_TOPK = 16
PEER_HALF = 64
CHUNK = 64
ROPE_THETA = 10000.0
LN_EPS = 1e-5
RMS_EPS = 1e-5
DEPTH = 2
DEEPNORM_ALPHA = (2.0 * DEPTH) ** 0.25

LANES = 128
SUM_ROWS = 16
LOG2E = math.log2(math.e)

PROJ_TM = 256
ATTN_TQ = 512
ATTN_TK = 256
SB_TQ = 512
SB_TK = 256
ROUTE_TM = 256
DENSE_TM = 512
DENSE_TE = 1024
VMEM_LIMIT = 56 * 1024 * 1024


def _cparams(sem):
    return pltpu.CompilerParams(dimension_semantics=sem, vmem_limit_bytes=VMEM_LIMIT)


def _proj_kernel(x_ref, w_ref, cos_ref, sina_ref, sinb_ref, o_ref, *, rope_groups, scale_groups, scale):
    x = x_ref[...].astype(BF16)
    acc = jnp.dot(x, w_ref[...], preferred_element_type=F32)
    n_groups = acc.shape[1] // LANES
    for g in range(n_groups):
        t = acc[:, g * LANES:(g + 1) * LANES]
        if g < rope_groups:
            t = (t * cos_ref[...] + pltpu.roll(t, LANES - 32, 1) * sina_ref[...]
                 + pltpu.roll(t, 32, 1) * sinb_ref[...])
        if g < scale_groups:
            t = t * scale
        o_ref[:, g * LANES:(g + 1) * LANES] = t.astype(o_ref.dtype)


def _proj(x2d, w_bf16, tables, *, seq, rope_groups, scale_groups, scale):
    T, D = x2d.shape
    N = w_bf16.shape[1]
    tm = PROJ_TM
    n_seq_blocks = seq // tm
    cos, sina, sinb = tables
    tab_spec = pl.BlockSpec((tm, LANES), lambda i: (i % n_seq_blocks, 0))
    return pl.pallas_call(
        functools.partial(_proj_kernel, rope_groups=rope_groups, scale_groups=scale_groups, scale=scale),
        grid=(T // tm,),
        in_specs=[pl.BlockSpec((tm, D), lambda i: (i, 0)),
                  pl.BlockSpec((D, N), lambda i: (0, 0)),
                  tab_spec, tab_spec, tab_spec],
        out_specs=pl.BlockSpec((tm, N), lambda i: (i, 0)),
        out_shape=jax.ShapeDtypeStruct((T, N), BF16),
        compiler_params=_cparams(("parallel",)),
        name="proj",
    )(x2d, w_bf16, cos, sina, sinb)


def _rope_tables(seq):
    pos = jnp.arange(seq, dtype=F32)
    inv = ROPE_THETA ** (-jnp.arange(0, DIFF_HEAD_DIM, 2, dtype=F32) / DIFF_HEAD_DIM)
    ang = pos[:, None] * inv[None, :]
    ang = jnp.concatenate([ang, ang, ang, ang], axis=-1)
    cos, sin = jnp.cos(ang), jnp.sin(ang)
    first_half = (jnp.arange(LANES) % DIFF_HEAD_DIM) < (DIFF_HEAD_DIM // 2)
    sina = jnp.where(first_half[None, :], -sin, 0.0)
    sinb = jnp.where(first_half[None, :], 0.0, sin)
    return cos, sina, sinb


def _transpose_value_blocks(v_ref, vt_ref, tk):
    for c in range(vt_ref.shape[0]):
        vt_ref[c] = v_ref[c * tk:(c + 1) * tk, :].astype(F32).T.astype(BF16)


def _split_halves_t(q_ref):
    qt = q_ref[...].astype(F32).T
    sub = lax.broadcasted_iota(jnp.int32, qt.shape, 0)
    first = jnp.where(sub < LANES // 2, qt, 0.0).astype(BF16)
    second = jnp.where(sub >= LANES // 2, qt, 0.0).astype(BF16)
    return first, second, sub


def _diff_attn_kernel(lp_ref, g_ref, q_ref, k_ref, v_ref, o_ref, vt_ref, m_ref, l_ref, acc_ref, s_ref, *,
                      tq, tk, lambda_init):
    i = pl.program_id(2)

    @pl.when(i == 0)
    def _():
        _transpose_value_blocks(v_ref, vt_ref, tk)

    q_maps = _split_halves_t(q_ref)[:2]

    m_ref[...] = jnp.full(m_ref.shape, -jnp.inf, F32)
    l_ref[...] = jnp.zeros(l_ref.shape, F32)
    acc_ref[...] = jnp.zeros(acc_ref.shape, F32)

    def scores(j, mi):
        start = pl.multiple_of(j * tk, tk)
        return jnp.dot(k_ref[pl.ds(start, tk), :], q_maps[mi], preferred_element_type=F32)

    def block_softmax(s, j, masked):
        if masked:
            key = j * tk + lax.broadcasted_iota(jnp.int32, s.shape, 0)
            qry = i * tq + lax.broadcasted_iota(jnp.int32, s.shape, 1)
            s = jnp.where((key // CHUNK) <= (qry // CHUNK), s, -jnp.inf)
        m_blk = jnp.max(s, axis=0, keepdims=True)
        m_ref_pt = jnp.where(m_blk == -jnp.inf, 0.0, m_blk) if masked else m_blk
        p = jnp.exp2(s - m_ref_pt)
        l_blk = jnp.sum(p, axis=0, keepdims=True)
        pv = jnp.dot(vt_ref[j], p.astype(BF16), preferred_element_type=F32)
        return m_blk, l_blk, pv

    def combine(mi, blk):
        m_blk, l_blk, pv = blk
        m_prev = m_ref[mi]
        m_new = jnp.maximum(m_prev, m_blk)
        a_prev = jnp.exp2(m_prev - m_new)
        a_blk = jnp.exp2(m_blk - m_new)
        l_ref[mi] = a_prev * l_ref[mi] + a_blk * l_blk
        acc_ref[mi] = a_prev * acc_ref[mi] + a_blk * pv
        m_ref[mi] = m_new

    assert tq == 2 * tk
    chains = [(d, mi) for d in range(2) for mi in range(2)]

    for d, mi in chains:
        combine(mi, block_softmax(scores(2 * i + d, mi), 2 * i + d, True))

    @pl.when(i > 0)
    def _():
        for c, (d, mi) in enumerate(chains):
            s_ref[c] = scores(2 * (i - 1) + d, mi)

        def body(t, carry):
            j_cur = 2 * (i - 1 - t)
            for c, (d, mi) in enumerate(chains):
                blk = block_softmax(s_ref[c], j_cur + d, False)
                s_ref[c] = scores(j_cur - 2 + d, mi)
                combine(mi, blk)
            return carry

        lax.fori_loop(0, i - 1, body, 0)
        for c, (d, mi) in enumerate(chains):
            combine(mi, block_softmax(s_ref[c], d, False))

    lp = lp_ref[...]
    s01 = jnp.sum(lp[0:1, :] * lp[1:2, :], axis=1, keepdims=True)
    s23 = jnp.sum(lp[2:3, :] * lp[3:4, :], axis=1, keepdims=True)
    lam = jnp.exp(s01) - jnp.exp(s23) + lambda_init
    o = acc_ref[0] * (1.0 / l_ref[0]) - lam * (acc_ref[1] * (1.0 / l_ref[1]))
    o = o * lax.rsqrt(jnp.mean(o * o, axis=0, keepdims=True) + RMS_EPS)
    o = o * g_ref[...] * (1.0 - lambda_init)
    o_ref[...] = o.T.astype(o_ref.dtype)


def _diff_attention(qkv, lam_params, subln_g, *, batch, seq, lambda_init):
    T = qkv.shape[0]
    tq, tk = ATTN_TQ, ATTN_TK
    nq = seq // tq
    H = DIFF_HEADS
    return pl.pallas_call(
        functools.partial(_diff_attn_kernel, tq=tq, tk=tk, lambda_init=lambda_init),
        grid=(batch, H, nq),
        in_specs=[pl.BlockSpec((4, DIFF_HEAD_DIM), lambda b, h, i: (0, 0)),
                  pl.BlockSpec((LANES, 1), lambda b, h, i: (0, 0)),
                  pl.BlockSpec((tq, LANES), lambda b, h, i: (b * nq + i, h)),
                  pl.BlockSpec((seq, LANES), lambda b, h, i: (b, H + h)),
                  pl.BlockSpec((seq, LANES), lambda b, h, i: (b, 2 * H + h))],
        out_specs=pl.BlockSpec((tq, LANES), lambda b, h, i: (b * nq + i, h)),
        out_shape=jax.ShapeDtypeStruct((T, H * LANES), BF16),
        scratch_shapes=[pltpu.VMEM((seq // tk, LANES, tk), BF16),
                        pltpu.VMEM((2, 1, tq), F32), pltpu.VMEM((2, 1, tq), F32),
                        pltpu.VMEM((2, LANES, tq), F32), pltpu.VMEM((4, tk, tq), F32)],
        compiler_params=_cparams(("parallel", "parallel", "arbitrary")),
        name="diff_attn",
    )(lam_params, subln_g.reshape(LANES, 1), qkv, qkv, qkv)


def _stick_kernel(q_ref, k_ref, v_ref, o_ref, vt_ref, r_ref, acc_ref, z_ref, *, tq, tk):
    i = pl.program_id(2)

    @pl.when(i == 0)
    def _():
        _transpose_value_blocks(v_ref, vt_ref, tk)

    q_first, q_second, sub = _split_halves_t(q_ref)
    q_heads = (q_first, q_second)

    jj = lax.broadcasted_iota(jnp.int32, (tk + SUM_ROWS, 2 * tk), 0)
    kk = lax.broadcasted_iota(jnp.int32, (tk + SUM_ROWS, 2 * tk), 1) % tk
    suffix = jnp.where((kk >= jj) | (jj >= tk), 1.0, 0.0).astype(BF16)

    r_ref[...] = jnp.zeros(r_ref.shape, F32)
    acc_ref[...] = jnp.zeros(acc_ref.shape, F32)

    def scores(jb, hd):
        start = pl.multiple_of(jb * tk, tk)
        return jnp.dot(k_ref[pl.ds(start, tk), :], q_heads[hd], preferred_element_type=F32)

    def visible(jb, shape):
        key = jb * tk + lax.broadcasted_iota(jnp.int32, shape, 0)
        qry = i * tq + lax.broadcasted_iota(jnp.int32, shape, 1)
        return key < qry

    def split_logs(z, keep):
        nz = -z
        lp = jnp.log2(1.0 + jnp.exp2(jnp.minimum(z, nz)))
        lsm = jnp.minimum(nz, 0.0) - lp
        if keep is not None:
            lsm = jnp.where(keep, lsm, 0.0)
        hi = lsm.astype(BF16)
        return jnp.concatenate([hi, (lsm - hi.astype(F32)).astype(BF16)], axis=0)

    def suffix_sums(hi_lo):
        return jnp.dot(suffix, hi_lo, preferred_element_type=F32)

    def weighted_values(z, sums, jb, keep):
        a = jnp.exp2(z + sums[0:tk])
        if keep is not None:
            a = jnp.where(keep, a, 0.0)
        pv = jnp.dot(vt_ref[jb], a.astype(BF16), preferred_element_type=F32)
        return pv, sums[tk:tk + 1]

    def combine(hd, blk):
        pv, total = blk
        acc_ref[hd] = acc_ref[hd] + jnp.exp2(r_ref[hd]) * pv
        r_ref[hd] = r_ref[hd] + total

    assert tq == 2 * tk
    chains = [(d, hd) for d in range(2) for hd in range(2)]

    for d, hd in chains:
        jb = 2 * i + 1 - d
        z = scores(jb, hd)
        keep = visible(jb, z.shape)
        combine(hd, weighted_values(z, suffix_sums(split_logs(z, keep)), jb, keep))

    def pipelined_pair(jb_right, prefetch):
        hi_lo, sums, blk = {}, {}, {}

        def logs(c):
            hi_lo[c] = split_logs(z_ref[c], None)
            sums[c] = suffix_sums(hi_lo[c])

        def values(c):
            d, hd = chains[c]
            blk[c] = weighted_values(z_ref[c], sums[c], jb_right - d, None)
            if prefetch:
                z_ref[c] = scores(jb_right - 2 - d, hd)

        logs(0)
        logs(1)
        values(0)
        logs(2)
        values(1)
        logs(3)
        values(2)
        values(3)
        for c, (d, hd) in enumerate(chains):
            combine(hd, blk[c])

    @pl.when(i > 0)
    def _():
        for c, (d, hd) in enumerate(chains):
            z_ref[c] = scores(2 * i - 1 - d, hd)

        def body(t, carry):
            pipelined_pair(2 * i - 1 - 2 * t, True)
            return carry

        lax.fori_loop(0, i - 1, body, 0)
        pipelined_pair(1, False)

    out = jnp.where(sub < SB_HEAD_DIM, acc_ref[0], acc_ref[1])
    o_ref[...] = out.T.astype(o_ref.dtype)


def _stick_breaking(qkv, *, batch, seq):
    T = qkv.shape[0]
    tq, tk = SB_TQ, SB_TK
    nq = seq // tq
    P = SB_HEADS // 2
    return pl.pallas_call(
        functools.partial(_stick_kernel, tq=tq, tk=tk),
        grid=(batch, P, nq),
        in_specs=[pl.BlockSpec((tq, LANES), lambda b, p, i: (b * nq + i, p)),
                  pl.BlockSpec((seq, LANES), lambda b, p, i: (b, P + p)),
                  pl.BlockSpec((seq, LANES), lambda b, p, i: (b, 2 * P + p))],
        out_specs=pl.BlockSpec((tq, LANES), lambda b, p, i: (b * nq + i, p)),
        out_shape=jax.ShapeDtypeStruct((T, P * LANES), BF16),
        scratch_shapes=[pltpu.VMEM((seq // tk, LANES, tk), BF16),
                        pltpu.VMEM((2, 1, tq), F32), pltpu.VMEM((2, LANES, tq), F32),
                        pltpu.VMEM((4, tk, tq), F32)],
        compiler_params=_cparams(("parallel", "parallel", "arbitrary")),
        name="stick_breaking",
    )(qkv, qkv, qkv)


def _layer_norm_rows(y, g, b):
    mu = jnp.mean(y, axis=1, keepdims=True)
    d = y - mu
    var = jnp.mean(d * d, axis=1, keepdims=True)
    return d * lax.rsqrt(var + LN_EPS) * g + b


def _proj_ln_kernel(a_ref, w_ref, x_ref, g_ref, b_ref, o_ref):
    y = jnp.dot(a_ref[...], w_ref[...], preferred_element_type=F32) + DEEPNORM_ALPHA * x_ref[...]
    o_ref[...] = _layer_norm_rows(y, g_ref[...], b_ref[...])


def _proj_ln(a_bf16, w_bf16, x2d, g, b):
    T, D = x2d.shape
    tm = PROJ_TM
    return pl.pallas_call(
        _proj_ln_kernel,
        grid=(T // tm,),
        in_specs=[pl.BlockSpec((tm, D), lambda i: (i, 0)),
                  pl.BlockSpec((D, D), lambda i: (0, 0)),
                  pl.BlockSpec((tm, D), lambda i: (i, 0)),
                  pl.BlockSpec((1, D), lambda i: (0, 0)),
                  pl.BlockSpec((1, D), lambda i: (0, 0))],
        out_specs=pl.BlockSpec((tm, D), lambda i: (i, 0)),
        out_shape=jax.ShapeDtypeStruct((T, D), F32),
        compiler_params=_cparams(("parallel",)),
        name="proj_ln",
    )(a_bf16, w_bf16, x2d, g.reshape(1, D), b.reshape(1, D))


N_TOP = PEER_TOPK + 1


def _extract_top(scores, n, with_rank):
    work = scores
    vals = []
    rank = jnp.full(scores.shape, float(PEER_N_KEYS), F32) if with_rank else None
    for r in range(n):
        m = jnp.max(work, axis=0, keepdims=True)
        vals.append(m)
        hit = work == m
        if with_rank:
            rank = jnp.where(hit, float(r), rank)
        work = jnp.where(hit, -jnp.inf, work)
    return vals, rank


def _route_kernel(x_ref, w_ref, kp_ref, rb_ref, e2_ref, n_ref, a1_ref):
    x = x_ref[...].astype(BF16)
    qp = jnp.dot(x, w_ref[...], preferred_element_type=F32).astype(BF16)
    tm = x.shape[0]
    for h in range(PEER_HEADS):
        qh = qp[:, h * LANES:(h + 1) * LANES]
        nt = (((1,), (1,)), ((), ()))
        s1 = lax.dot_general(kp_ref[h, 0], qh, nt, preferred_element_type=F32)
        s2 = lax.dot_general(kp_ref[h, 1], qh, nt, preferred_element_type=F32)
        top1, _ = _extract_top(s1, N_TOP, False)
        top2, rank2 = _extract_top(s2, N_TOP, True)
        cands = [top1[i] + top2[j] for i in range(N_TOP) for j in range(N_TOP)
                 if (i + 1) * (j + 1) <= N_TOP]
        pad = (-len(cands)) % 8
        cands += [jnp.full((1, tm), -jnp.inf, F32)] * pad
        ctop, _ = _extract_top(jnp.concatenate(cands, axis=0), N_TOP, False)
        cut = 0.5 * (ctop[PEER_TOPK - 1] + ctop[PEER_TOPK])
        cmax = ctop[0]
        z = ctop[0] - cmax
        z = jnp.exp(z)
        for r in range(1, PEER_TOPK):
            z = z + jnp.exp(ctop[r] - cmax)
        count1 = jnp.zeros(s1.shape, F32)
        for j in range(PEER_TOPK):
            count1 = count1 + jnp.where(s1 + top2[j] >= cut, 1.0, 0.0)
        rb_ref[h] = rank2.astype(BF16)
        e2_ref[h] = jnp.exp(s2 - top2[0]).astype(BF16)
        n_ref[h] = count1
        a1_ref[h] = jnp.exp(s1 - top1[0]) / z


def _peer_route(x2d, w_pq_bf16, keys_padded):
    T, D = x2d.shape
    tm = ROUTE_TM
    out_spec = pl.BlockSpec((PEER_HEADS, PEER_N_KEYS, tm), lambda i: (0, 0, i))
    sds = lambda dtype: jax.ShapeDtypeStruct((PEER_HEADS, PEER_N_KEYS, T), dtype)
    return pl.pallas_call(
        _route_kernel,
        grid=(T // tm,),
        in_specs=[pl.BlockSpec((tm, D), lambda i: (i, 0)),
                  pl.BlockSpec((D, PEER_HEADS * LANES), lambda i: (0, 0)),
                  pl.BlockSpec((PEER_HEADS, 2, PEER_N_KEYS, LANES), lambda i: (0, 0, 0, 0))],
        out_specs=[out_spec] * 4,
        out_shape=[sds(BF16), sds(BF16), sds(F32), sds(F32)],
        compiler_params=_cparams(("parallel",)),
        name="peer_route",
    )(x2d, w_pq_bf16, keys_padded)


def _peer_dense_kernel(x_ref, u_ref, vta_ref, vtb_ref, rb_ref, e2_ref, na_ref, a1a_ref, nb_ref, a1b_ref,
                       g_ref, b_ref, o_ref, xt_ref, acc_ref, ht_a_ref, ht_b_ref, gs_ref, *, te):
    s = pl.program_id(1)
    last = pl.num_programs(1) - 1

    @pl.when(s == 0)
    def _():
        xt_ref[...] = x_ref[...].T.astype(BF16)
        acc_ref[...] = jnp.zeros(acc_ref.shape, F32)
        ht_b_ref[...] = jnp.zeros(ht_b_ref.shape, F32)

    def half_step(half, n_ref, a1_ref, vt_ref, ht_out, ht_in, live):
        ht_out[...] = jnp.dot(u_ref[half * te:(half + 1) * te, :], xt_ref[...],
                              preferred_element_type=F32)
        for i in range(te // PEER_N_KEYS):
            rows = slice(i * PEER_N_KEYS, (i + 1) * PEER_N_KEYS)
            w = None
            for h in range(PEER_HEADS):
                cnt = n_ref[h, i:i + 1, :].astype(BF16)
                a1 = a1_ref[h, i:i + 1, :].astype(BF16)
                if live is not None:
                    a1 = a1 * live
                term = a1 * jnp.where(rb_ref[h] < cnt, e2_ref[h], 0)
                w = term if w is None else w + term
            hb = ht_in[rows, :].astype(BF16)
            gelu = (0.5 * hb) * (1.0 + lax.erf(hb * (1.0 / math.sqrt(2.0))))
            gs_ref[rows, :] = w * gelu
        acc_ref[...] += jnp.dot(vt_ref[...], gs_ref[...], preferred_element_type=F32)

    half_step(0, na_ref, a1a_ref, vta_ref, ht_a_ref, ht_b_ref, None)
    not_draining = jnp.where(s < last, 1.0, 0.0).astype(BF16)
    half_step(1, nb_ref, a1b_ref, vtb_ref, ht_b_ref, ht_a_ref, not_draining)

    @pl.when(s == last)
    def _():
        y = acc_ref[...].T + DEEPNORM_ALPHA * x_ref[...]
        o_ref[...] = _layer_norm_rows(y, g_ref[...], b_ref[...])


def _peer_dense(x2d, u_bf16, vt_bf16, routing, g, b):
    T, D = x2d.shape
    E = u_bf16.shape[0]
    tm, te = DENSE_TM, DENSE_TE
    n_tiles = E // te
    n_pairs = n_tiles // 2
    rb, e2, cnt, a1 = routing
    tile_a = lambda s: jnp.clip(2 * s - 1, 0, n_tiles - 1)
    tile_b = lambda s: jnp.clip(2 * s, 0, n_tiles - 1)
    full_spec = pl.BlockSpec((PEER_HEADS, PEER_N_KEYS, tm), lambda t, s: (0, 0, t))
    rows_a = pl.BlockSpec((PEER_HEADS, te // PEER_N_KEYS, tm), lambda t, s: (0, tile_a(s), t))
    rows_b = pl.BlockSpec((PEER_HEADS, te // PEER_N_KEYS, tm), lambda t, s: (0, tile_b(s), t))
    return pl.pallas_call(
        functools.partial(_peer_dense_kernel, te=te),
        grid=(T // tm, n_pairs + 1),
        in_specs=[pl.BlockSpec((tm, D), lambda t, s: (t, 0)),
                  pl.BlockSpec((2 * te, D), lambda t, s: (jnp.minimum(s, n_pairs - 1), 0)),
                  pl.BlockSpec((D, te), lambda t, s: (0, tile_a(s))),
                  pl.BlockSpec((D, te), lambda t, s: (0, tile_b(s))),
                  full_spec, full_spec, rows_a, rows_a, rows_b, rows_b,
                  pl.BlockSpec((1, D), lambda t, s: (0, 0)),
                  pl.BlockSpec((1, D), lambda t, s: (0, 0))],
        out_specs=pl.BlockSpec((tm, D), lambda t, s: (t, 0)),
        out_shape=jax.ShapeDtypeStruct((T, D), F32),
        scratch_shapes=[pltpu.VMEM((D, tm), BF16), pltpu.VMEM((D, tm), F32),
                        pltpu.VMEM((te, tm), F32), pltpu.VMEM((te, tm), F32),
                        pltpu.VMEM((te, tm), BF16)],
        compiler_params=_cparams(("parallel", "arbitrary")),
        name="peer_dense",
    )(x2d, u_bf16, vt_bf16, vt_bf16, rb, e2, cnt, a1, cnt, a1, g.reshape(1, D), b.reshape(1, D))


def _pad_sub_keys(sub_keys):
    z = jnp.zeros_like(sub_keys[:, 0])
    first = jnp.concatenate([sub_keys[:, 0], z], axis=-1)
    second = jnp.concatenate([z, sub_keys[:, 1]], axis=-1)
    return jnp.stack([first, second], axis=1).astype(BF16)


def _peer(x2d, w_pq, sub_keys, u, v, g, b):
    routing = _peer_route(x2d, w_pq.astype(BF16), _pad_sub_keys(sub_keys))
    return _peer_dense(x2d, u.astype(BF16), v.T.astype(BF16), routing, g, b)


def kernel(x, ln_g, ln_b, w_qkv_a, w_o_a, lambda_qk_a, subln_g_a, w_kv_b, w_q_b, w_o_b,
           peer_w_q, peer_sub_keys, peer_u, peer_v):
    B, S, D = x.shape
    x2d = x.reshape(B * S, D)
    tables = _rope_tables(S)
    n_qk_groups = 2 * D // LANES

    lambda_init = 0.8 - 0.6 * math.exp(-0.3 * 0)
    qkv = _proj(x2d, w_qkv_a[0].astype(BF16), tables, seq=S, rope_groups=n_qk_groups,
                scale_groups=D // LANES, scale=DIFF_HEAD_DIM ** -0.5 * LOG2E)
    att = _diff_attention(qkv, lambda_qk_a[0], subln_g_a[0], batch=B, seq=S, lambda_init=lambda_init)
    x2d = _proj_ln(att, w_o_a[0].astype(BF16), x2d, ln_g[0, 0], ln_b[0, 0])
    x2d = _peer(x2d, peer_w_q[0], peer_sub_keys[0], peer_u[0], peer_v[0], ln_g[0, 1], ln_b[0, 1])

    w_qkv_b = jnp.concatenate([w_q_b[0], w_kv_b], axis=1).astype(BF16)
    qkv = _proj(x2d, w_qkv_b, tables, seq=S, rope_groups=0,
                scale_groups=D // LANES, scale=SB_HEAD_DIM ** -0.5 * LOG2E)
    sb = _stick_breaking(qkv, batch=B, seq=S)
    x2d = _proj_ln(sb, w_o_b[0].astype(BF16), x2d, ln_g[1, 0], ln_b[1, 0])
    x2d = _peer(x2d, peer_w_q[1], peer_sub_keys[1], peer_u[1], peer_v[1], ln_g[1, 1], ln_b[1, 1])
    return x2d.reshape(B, S, D)
```

```python
import functools
import math

import jax
import jax.numpy as jnp
from jax import lax
from jax.experimental import pallas as pl
from jax.experimental.pallas import tpu as pltpu

F32 = jnp.float32
BF16 = jnp.bfloat16

DIFF_HEADS = 8
DIFF_HEAD_DIM = 64
SB_HEADS = 16
SB_HEAD_DIM = 64
PEER_HEADS = 8
PEER_N_KEYS = 128
PEER_TOPK = 16
PEER_HALF = 64
CHUNK = 64
ROPE_THETA = 10000.0
LN_EPS = 1e-5
RMS_EPS = 1e-5
DEPTH = 2
DEEPNORM_ALPHA = (2.0 * DEPTH) ** 0.25

LANES = 128
SUM_ROWS = 16
LOG2E = math.log2(math.e)

PROJ_TM = 256
ATTN_TQ = 512
ATTN_TK = 256
SB_TQ = 512
SB_TK = 256
ROUTE_TM = 256
DENSE_TM = 512
DENSE_TE = 1024
VMEM_LIMIT = 56 * 1024 * 1024


def _cparams(sem):
    return pltpu.CompilerParams(dimension_semantics=sem, vmem_limit_bytes=VMEM_LIMIT)


def _proj_kernel(x_ref, w_ref, cos_ref, sina_ref, sinb_ref, o_ref, *, rope_groups, scale_groups, scale):
    x = x_ref[...].astype(BF16)
    acc = jnp.dot(x, w_ref[...], preferred_element_type=F32)
    n_groups = acc.shape[1] // LANES
    for g in range(n_groups):
        t = acc[:, g * LANES:(g + 1) * LANES]
        if g < rope_groups:
            t = (t * cos_ref[...] + pltpu.roll(t, LANES - 32, 1) * sina_ref[...]
                 + pltpu.roll(t, 32, 1) * sinb_ref[...])
        if g < scale_groups:
            t = t * scale
        o_ref[:, g * LANES:(g + 1) * LANES] = t.astype(o_ref.dtype)


def _proj(x2d, w_bf16, tables, *, seq, rope_groups, scale_groups, scale):
    T, D = x2d.shape
    N = w_bf16.shape[1]
    tm = PROJ_TM
    n_seq_blocks = seq // tm
    cos, sina, sinb = tables
    tab_spec = pl.BlockSpec((tm, LANES), lambda i: (i % n_seq_blocks, 0))
    return pl.pallas_call(
        functools.partial(_proj_kernel, rope_groups=rope_groups, scale_groups=scale_groups, scale=scale),
        grid=(T // tm,),
        in_specs=[pl.BlockSpec((tm, D), lambda i: (i, 0)),
                  pl.BlockSpec((D, N), lambda i: (0, 0)),
                  tab_spec, tab_spec, tab_spec],
        out_specs=pl.BlockSpec((tm, N), lambda i: (i, 0)),
        out_shape=jax.ShapeDtypeStruct((T, N), BF16),
        compiler_params=_cparams(("parallel",)),
        name="proj",
    )(x2d, w_bf16, cos, sina, sinb)


def _rope_tables(seq):
    pos = jnp.arange(seq, dtype=F32)
    inv = ROPE_THETA ** (-jnp.arange(0, DIFF_HEAD_DIM, 2, dtype=F32) / DIFF_HEAD_DIM)
    ang = pos[:, None] * inv[None, :]
    ang = jnp.concatenate([ang, ang, ang, ang], axis=-1)
    cos, sin = jnp.cos(ang), jnp.sin(ang)
    first_half = (jnp.arange(LANES) % DIFF_HEAD_DIM) < (DIFF_HEAD_DIM // 2)
    sina = jnp.where(first_half[None, :], -sin, 0.0)
    sinb = jnp.where(first_half[None, :], 0.0, sin)
    return cos, sina, sinb


def _transpose_value_blocks(v_ref, vt_ref, tk):
    for c in range(vt_ref.shape[0]):
        vt_ref[c] = v_ref[c * tk:(c + 1) * tk, :].astype(F32).T.astype(BF16)


def _split_halves_t(q_ref):
    qt = q_ref[...].astype(F32).T
    sub = lax.broadcasted_iota(jnp.int32, qt.shape, 0)
    first = jnp.where(sub < LANES // 2, qt, 0.0).astype(BF16)
    second = jnp.where(sub >= LANES // 2, qt, 0.0).astype(BF16)
    return first, second, sub


def _diff_attn_kernel(lp_ref, g_ref, q_ref, k_ref, v_ref, o_ref, vt_ref, m_ref, l_ref, acc_ref, s_ref, *,
                      tq, tk, lambda_init):
    i = pl.program_id(2)

    @pl.when(i == 0)
    def _():
        _transpose_value_blocks(v_ref, vt_ref, tk)

    q_maps = _split_halves_t(q_ref)[:2]

    m_ref[...] = jnp.full(m_ref.shape, -jnp.inf, F32)
    l_ref[...] = jnp.zeros(l_ref.shape, F32)
    acc_ref[...] = jnp.zeros(acc_ref.shape, F32)

    def scores(j, mi):
        start = pl.multiple_of(j * tk, tk)
        return jnp.dot(k_ref[pl.ds(start, tk), :], q_maps[mi], preferred_element_type=F32)

    def block_softmax(s, j, masked):
        if masked:
            key = j * tk + lax.broadcasted_iota(jnp.int32, s.shape, 0)
            qry = i * tq + lax.broadcasted_iota(jnp.int32, s.shape, 1)
            s = jnp.where((key // CHUNK) <= (qry // CHUNK), s, -jnp.inf)
        m_blk = jnp.max(s, axis=0, keepdims=True)
        m_ref_pt = jnp.where(m_blk == -jnp.inf, 0.0, m_blk) if masked else m_blk
        p = jnp.exp2(s - m_ref_pt)
        l_blk = jnp.sum(p, axis=0, keepdims=True)
        pv = jnp.dot(vt_ref[j], p.astype(BF16), preferred_element_type=F32)
        return m_blk, l_blk, pv

    def combine(mi, blk):
        m_blk, l_blk, pv = blk
        m_prev = m_ref[mi]
        m_new = jnp.maximum(m_prev, m_blk)
        a_prev = jnp.exp2(m_prev - m_new)
        a_blk = jnp.exp2(m_blk - m_new)
        l_ref[mi] = a_prev * l_ref[mi] + a_blk * l_blk
        acc_ref[mi] = a_prev * acc_ref[mi] + a_blk * pv
        m_ref[mi] = m_new

    assert tq == 2 * tk
    chains = [(d, mi) for d in range(2) for mi in range(2)]

    for d, mi in chains:
        combine(mi, block_softmax(scores(2 * i + d, mi), 2 * i + d, True))

    @pl.when(i > 0)
    def _():
        for c, (d, mi) in enumerate(chains):
            s_ref[c] = scores(2 * (i - 1) + d, mi)

        def body(t, carry):
            j_cur = 2 * (i - 1 - t)
            for c, (d, mi) in enumerate(chains):
                blk = block_softmax(s_ref[c], j_cur + d, False)
                s_ref[c] = scores(j_cur - 2 + d, mi)
                combine(mi, blk)
            return carry

        lax.fori_loop(0, i - 1, body, 0)
        for c, (d, mi) in enumerate(chains):
            combine(mi, block_softmax(s_ref[c], d, False))

    lp = lp_ref[...]
    s01 = jnp.sum(lp[0:1, :] * lp[1:2, :], axis=1, keepdims=True)
    s23 = jnp.sum(lp[2:3, :] * lp[3:4, :], axis=1, keepdims=True)
    lam = jnp.exp(s01) - jnp.exp(s23) + lambda_init
    o = acc_ref[0] * (1.0 / l_ref[0]) - lam * (acc_ref[1] * (1.0 / l_ref[1]))
    o = o * lax.rsqrt(jnp.mean(o * o, axis=0, keepdims=True) + RMS_EPS)
    o = o * g_ref[...] * (1.0 - lambda_init)
    o_ref[...] = o.T.astype(o_ref.dtype)


def _diff_attention(qkv, lam_params, subln_g, *, batch, seq, lambda_init):
    T = qkv.shape[0]
    tq, tk = ATTN_TQ, ATTN_TK
    nq = seq // tq
    H = DIFF_HEADS
    return pl.pallas_call(
        functools.partial(_diff_attn_kernel, tq=tq, tk=tk, lambda_init=lambda_init),
        grid=(batch, H, nq),
        in_specs=[pl.BlockSpec((4, DIFF_HEAD_DIM), lambda b, h, i: (0, 0)),
                  pl.BlockSpec((LANES, 1), lambda b, h, i: (0, 0)),
                  pl.BlockSpec((tq, LANES), lambda b, h, i: (b * nq + i, h)),
                  pl.BlockSpec((seq, LANES), lambda b, h, i: (b, H + h)),
                  pl.BlockSpec((seq, LANES), lambda b, h, i: (b, 2 * H + h))],
        out_specs=pl.BlockSpec((tq, LANES), lambda b, h, i: (b * nq + i, h)),
        out_shape=jax.ShapeDtypeStruct((T, H * LANES), BF16),
        scratch_shapes=[pltpu.VMEM((seq // tk, LANES, tk), BF16),
                        pltpu.VMEM((2, 1, tq), F32), pltpu.VMEM((2, 1, tq), F32),
                        pltpu.VMEM((2, LANES, tq), F32), pltpu.VMEM((4, tk, tq), F32)],
        compiler_params=_cparams(("parallel", "parallel", "arbitrary")),
        name="diff_attn",
    )(lam_params, subln_g.reshape(LANES, 1), qkv, qkv, qkv)


def _stick_kernel(q_ref, k_ref, v_ref, o_ref, vt_ref, r_ref, acc_ref, z_ref, *, tq, tk):
    i = pl.program_id(2)

    @pl.when(i == 0)
    def _():
        _transpose_value_blocks(v_ref, vt_ref, tk)

    q_first, q_second, sub = _split_halves_t(q_ref)
    q_heads = (q_first, q_second)

    jj = lax.broadcasted_iota(jnp.int32, (tk + SUM_ROWS, tk), 0)
    kk = lax.broadcasted_iota(jnp.int32, (tk + SUM_ROWS, tk), 1)
    suffix = jnp.where((kk >= jj) | (jj >= tk), 1.0, 0.0).astype(BF16)

    r_ref[...] = jnp.zeros(r_ref.shape, F32)
    acc_ref[...] = jnp.zeros(acc_ref.shape, F32)

    def scores(jb, hd):
        start = pl.multiple_of(jb * tk, tk)
        return jnp.dot(k_ref[pl.ds(start, tk), :], q_heads[hd], preferred_element_type=F32)

    def visible(jb, shape):
        key = jb * tk + lax.broadcasted_iota(jnp.int32, shape, 0)
        qry = i * tq + lax.broadcasted_iota(jnp.int32, shape, 1)
        return key < qry

    def split_logs(z, keep):
        nz = -z
        lp = jnp.log2(1.0 + jnp.exp2(jnp.minimum(z, nz)))
        lsm = jnp.minimum(nz, 0.0) - lp
        if keep is not None:
            lsm = jnp.where(keep, lsm, 0.0)
        return lsm.astype(BF16)

    def suffix_sums(hi_lo):
        return jnp.dot(suffix, hi_lo, preferred_element_type=F32)

    def weighted_values(z, sums, jb, keep):
        a = jnp.exp2(z + sums[0:tk])
        if keep is not None:
            a = jnp.where(keep, a, 0.0)
        pv = jnp.dot(vt_ref[jb], a.astype(BF16), preferred_element_type=F32)
        return pv, sums[tk:tk + 1]

    def combine(hd, blk):
        pv, total = blk
        acc_ref[hd] = acc_ref[hd] + jnp.exp2(r_ref[hd]) * pv
        r_ref[hd] = r_ref[hd] + total

    assert tq == 2 * tk
    chains = [(d, hd) for d in range(2) for hd in range(2)]

    for d, hd in chains:
        jb = 2 * i + 1 - d
        z = scores(jb, hd)
        keep = visible(jb, z.shape)
        combine(hd, weighted_values(z, suffix_sums(split_logs(z, keep)), jb, keep))

    def pipelined_pair(jb_right, prefetch):
        hi_lo, sums, blk = {}, {}, {}

        def logs(c):
            hi_lo[c] = split_logs(z_ref[c], None)
            sums[c] = suffix_sums(hi_lo[c])

        def values(c):
            d, hd = chains[c]
            blk[c] = weighted_values(z_ref[c], sums[c], jb_right - d, None)
            if prefetch:
                z_ref[c] = scores(jb_right - 2 - d, hd)

        logs(0)
        logs(1)
        values(0)
        logs(2)
        values(1)
        logs(3)
        values(2)
        values(3)
        for c, (d, hd) in enumerate(chains):
            combine(hd, blk[c])

    @pl.when(i > 0)
    def _():
        for c, (d, hd) in enumerate(chains):
            z_ref[c] = scores(2 * i - 1 - d, hd)

        def body(t, carry):
            pipelined_pair(2 * i - 1 - 2 * t, True)
            return carry

        lax.fori_loop(0, i - 1, body, 0)
        pipelined_pair(1, False)

    out = jnp.where(sub < SB_HEAD_DIM, acc_ref[0], acc_ref[1])
    o_ref[...] = out.T.astype(o_ref.dtype)


def _stick_breaking(qkv, *, batch, seq):
    T = qkv.shape[0]
    tq, tk = SB_TQ, SB_TK
    nq = seq // tq
    P = SB_HEADS // 2
    return pl.pallas_call(
        functools.partial(_stick_kernel, tq=tq, tk=tk),
        grid=(batch, P, nq),
        in_specs=[pl.BlockSpec((tq, LANES), lambda b, p, i: (b * nq + i, p)),
                  pl.BlockSpec((seq, LANES), lambda b, p, i: (b, P + p)),
                  pl.BlockSpec((seq, LANES), lambda b, p, i: (b, 2 * P + p))],
        out_specs=pl.BlockSpec((tq, LANES), lambda b, p, i: (b * nq + i, p)),
        out_shape=jax.ShapeDtypeStruct((T, P * LANES), BF16),
        scratch_shapes=[pltpu.VMEM((seq // tk, LANES, tk), BF16),
                        pltpu.VMEM((2, 1, tq), F32), pltpu.VMEM((2, LANES, tq), F32),
                        pltpu.VMEM((4, tk, tq), F32)],
        compiler_params=_cparams(("parallel", "parallel", "arbitrary")),
        name="stick_breaking",
    )(qkv, qkv, qkv)


def _layer_norm_rows(y, g, b):
    mu = jnp.mean(y, axis=1, keepdims=True)
    d = y - mu
    var = jnp.mean(d * d, axis=1, keepdims=True)
    return d * lax.rsqrt(var + LN_EPS) * g + b


def _proj_ln_kernel(a_ref, w_ref, x_ref, g_ref, b_ref, o_ref):
    y = jnp.dot(a_ref[...], w_ref[...], preferred_element_type=F32) + DEEPNORM_ALPHA * x_ref[...]
    o_ref[...] = _layer_norm_rows(y, g_ref[...], b_ref[...])


def _proj_ln(a_bf16, w_bf16, x2d, g, b):
    T, D = x2d.shape
    tm = PROJ_TM
    return pl.pallas_call(
        _proj_ln_kernel,
        grid=(T // tm,),
        in_specs=[pl.BlockSpec((tm, D), lambda i: (i, 0)),
                  pl.BlockSpec((D, D), lambda i: (0, 0)),
                  pl.BlockSpec((tm, D), lambda i: (i, 0)),
                  pl.BlockSpec((1, D), lambda i: (0, 0)),
                  pl.BlockSpec((1, D), lambda i: (0, 0))],
        out_specs=pl.BlockSpec((tm, D), lambda i: (i, 0)),
        out_shape=jax.ShapeDtypeStruct((T, D), F32),
        compiler_params=_cparams(("parallel",)),
        name="proj_ln",
    )(a_bf16, w_bf16, x2d, g.reshape(1, D), b.reshape(1, D))


N_TOP = PEER_TOPK + 1


def _extract_top(scores, n, with_rank):
    work = scores
    vals = []
    rank = jnp.full(scores.shape, float(PEER_N_KEYS), F32) if with_rank else None
    for r in range(n):
        m = jnp.max(work, axis=0, keepdims=True)
        vals.append(m)
        hit = work == m
        if with_rank:
            rank = jnp.where(hit, float(r), rank)
        work = jnp.where(hit, -jnp.inf, work)
    return vals, rank


def _route_kernel(x_ref, w_ref, kp_ref, rb_ref, e2_ref, n_ref, a1_ref):
    x = x_ref[...].astype(BF16)
    qp = jnp.dot(x, w_ref[...], preferred_element_type=F32).astype(BF16)
    tm = x.shape[0]
    for h in range(PEER_HEADS):
        qh = qp[:, h * LANES:(h + 1) * LANES]
        nt = (((1,), (1,)), ((), ()))
        s1 = lax.dot_general(kp_ref[h, 0], qh, nt, preferred_element_type=F32)
        s2 = lax.dot_general(kp_ref[h, 1], qh, nt, preferred_element_type=F32)
        top1, _ = _extract_top(s1, N_TOP, False)
        top2, rank2 = _extract_top(s2, N_TOP, True)
        cands = [top1[i] + top2[j] for i in range(N_TOP) for j in range(N_TOP)
                 if (i + 1) * (j + 1) <= N_TOP]
        pad = (-len(cands)) % 8
        cands += [jnp.full((1, tm), -jnp.inf, F32)] * pad
        ctop, _ = _extract_top(jnp.concatenate(cands, axis=0), N_TOP, False)
        cut = 0.5 * (ctop[PEER_TOPK - 1] + ctop[PEER_TOPK])
        cmax = ctop[0]
        z = ctop[0] - cmax
        z = jnp.exp(z)
        for r in range(1, PEER_TOPK):
            z = z + jnp.exp(ctop[r] - cmax)
        count1 = jnp.zeros(s1.shape, F32)
        for j in range(PEER_TOPK):
            count1 = count1 + jnp.where(s1 + top2[j] >= cut, 1.0, 0.0)
        rb_ref[h] = rank2.astype(BF16)
        e2_ref[h] = jnp.exp(s2 - top2[0]).astype(BF16)
        n_ref[h] = count1
        a1_ref[h] = jnp.exp(s1 - top1[0]) / z


def _peer_route(x2d, w_pq_bf16, keys_padded):
    T, D = x2d.shape
    tm = ROUTE_TM
    out_spec = pl.BlockSpec((PEER_HEADS, PEER_N_KEYS, tm), lambda i: (0, 0, i))
    sds = lambda dtype: jax.ShapeDtypeStruct((PEER_HEADS, PEER_N_KEYS, T), dtype)
    return pl.pallas_call(
        _route_kernel,
        grid=(T // tm,),
        in_specs=[pl.BlockSpec((tm, D), lambda i: (i, 0)),
                  pl.BlockSpec((D, PEER_HEADS * LANES), lambda i: (0, 0)),
                  pl.BlockSpec((PEER_HEADS, 2, PEER_N_KEYS, LANES), lambda i: (0, 0, 0, 0))],
        out_specs=[out_spec] * 4,
        out_shape=[sds(BF16), sds(BF16), sds(F32), sds(F32)],
        compiler_params=_cparams(("parallel",)),
        name="peer_route",
    )(x2d, w_pq_bf16, keys_padded)


def _peer_dense_kernel(x_ref, u_ref, vta_ref, vtb_ref, rb_ref, e2_ref, na_ref, a1a_ref, nb_ref, a1b_ref,
                       g_ref, b_ref, o_ref, xt_ref, acc_ref, ht_a_ref, ht_b_ref, gs_ref, *, te):
    s = pl.program_id(1)
    last = pl.num_programs(1) - 1

    @pl.when(s == 0)
    def _():
        xt_ref[...] = x_ref[...].T.astype(BF16)
        acc_ref[...] = jnp.zeros(acc_ref.shape, F32)
        ht_b_ref[...] = jnp.zeros(ht_b_ref.shape, F32)

    def half_step(half, n_ref, a1_ref, vt_ref, ht_out, ht_in, live):
        ht_out[...] = jnp.dot(u_ref[half * te:(half + 1) * te, :], xt_ref[...],
                              preferred_element_type=F32)
        for i in range(te // PEER_N_KEYS):
            rows = slice(i * PEER_N_KEYS, (i + 1) * PEER_N_KEYS)
            w = None
            for h in range(PEER_HEADS):
                cnt = n_ref[h, i:i + 1, :].astype(BF16)
                a1 = a1_ref[h, i:i + 1, :].astype(BF16)
                if live is not None:
                    a1 = a1 * live
                term = a1 * jnp.where(rb_ref[h] < cnt, e2_ref[h], 0)
                w = term if w is None else w + term
            hb = ht_in[rows, :].astype(BF16)
            gelu = (0.5 * hb) * (1.0 + lax.erf(hb * (1.0 / math.sqrt(2.0))))
            gs_ref[rows, :] = w * gelu
        acc_ref[...] += jnp.dot(vt_ref[...], gs_ref[...], preferred_element_type=F32)

    half_step(0, na_ref, a1a_ref, vta_ref, ht_a_ref, ht_b_ref, None)
    not_draining = jnp.where(s < last, 1.0, 0.0).astype(BF16)
    half_step(1, nb_ref, a1b_ref, vtb_ref, ht_b_ref, ht_a_ref, not_draining)

    @pl.when(s == last)
    def _():
        y = acc_ref[...].T + DEEPNORM_ALPHA * x_ref[...]
        o_ref[...] = _layer_norm_rows(y, g_ref[...], b_ref[...])


def _peer_dense(x2d, u_bf16, vt_bf16, routing, g, b):
    T, D = x2d.shape
    E = u_bf16.shape[0]
    tm, te = DENSE_TM, DENSE_TE
    n_tiles = E // te
    n_pairs = n_tiles // 2
    rb, e2, cnt, a1 = routing
    tile_a = lambda s: jnp.clip(2 * s - 1, 0, n_tiles - 1)
    tile_b = lambda s: jnp.clip(2 * s, 0, n_tiles - 1)
    full_spec = pl.BlockSpec((PEER_HEADS, PEER_N_KEYS, tm), lambda t, s: (0, 0, t))
    rows_a = pl.BlockSpec((PEER_HEADS, te // PEER_N_KEYS, tm), lambda t, s: (0, tile_a(s), t))
    rows_b = pl.BlockSpec((PEER_HEADS, te // PEER_N_KEYS, tm), lambda t, s: (0, tile_b(s), t))
    return pl.pallas_call(
        functools.partial(_peer_dense_kernel, te=te),
        grid=(T // tm, n_pairs + 1),
        in_specs=[pl.BlockSpec((tm, D), lambda t, s: (t, 0)),
                  pl.BlockSpec((2 * te, D), lambda t, s: (jnp.minimum(s, n_pairs - 1), 0)),
                  pl.BlockSpec((D, te), lambda t, s: (0, tile_a(s))),
                  pl.BlockSpec((D, te), lambda t, s: (0, tile_b(s))),
                  full_spec, full_spec, rows_a, rows_a, rows_b, rows_b,
                  pl.BlockSpec((1, D), lambda t, s: (0, 0)),
                  pl.BlockSpec((1, D), lambda t, s: (0, 0))],
        out_specs=pl.BlockSpec((tm, D), lambda t, s: (t, 0)),
        out_shape=jax.ShapeDtypeStruct((T, D), F32),
        scratch_shapes=[pltpu.VMEM((D, tm), BF16), pltpu.VMEM((D, tm), F32),
                        pltpu.VMEM((te, tm), F32), pltpu.VMEM((te, tm), F32),
                        pltpu.VMEM((te, tm), BF16)],
        compiler_params=_cparams(("parallel", "arbitrary")),
        name="peer_dense",
    )(x2d, u_bf16, vt_bf16, vt_bf16, rb, e2, cnt, a1, cnt, a1, g.reshape(1, D), b.reshape(1, D))


def _pad_sub_keys(sub_keys):
    z = jnp.zeros_like(sub_keys[:, 0])
    first = jnp.concatenate([sub_keys[:, 0], z], axis=-1)
    second = jnp.concatenate([z, sub_keys[:, 1]], axis=-1)
    return jnp.stack([first, second], axis=1).astype(BF16)


def _peer(x2d, w_pq, sub_keys, u, v, g, b):
    routing = _peer_route(x2d, w_pq.astype(BF16), _pad_sub_keys(sub_keys))
    return _peer_dense(x2d, u.astype(BF16), v.T.astype(BF16), routing, g, b)


def kernel(x, ln_g, ln_b, w_qkv_a, w_o_a, lambda_qk_a, subln_g_a, w_kv_b, w_q_b, w_o_b,
           peer_w_q, peer_sub_keys, peer_u, peer_v):
    B, S, D = x.shape
    x2d = x.reshape(B * S, D)
    tables = _rope_tables(S)
    n_qk_groups = 2 * D // LANES

    lambda_init = 0.8 - 0.6 * math.exp(-0.3 * 0)
    qkv = _proj(x2d, w_qkv_a[0].astype(BF16), tables, seq=S, rope_groups=n_qk_groups,
                scale_groups=D // LANES, scale=DIFF_HEAD_DIM ** -0.5 * LOG2E)
    att = _diff_attention(qkv, lambda_qk_a[0], subln_g_a[0], batch=B, seq=S, lambda_init=lambda_init)
    x2d = _proj_ln(att, w_o_a[0].astype(BF16), x2d, ln_g[0, 0], ln_b[0, 0])
    x2d = _peer(x2d, peer_w_q[0], peer_sub_keys[0], peer_u[0], peer_v[0], ln_g[0, 1], ln_b[0, 1])

    w_qkv_b = jnp.concatenate([w_q_b[0], w_kv_b], axis=1).astype(BF16)
    qkv = _proj(x2d, w_qkv_b, tables, seq=S, rope_groups=0,
                scale_groups=D // LANES, scale=SB_HEAD_DIM ** -0.5 * LOG2E)
    sb = _stick_breaking(qkv, batch=B, seq=S)
    x2d = _proj_ln(sb, w_o_b[0].astype(BF16), x2d, ln_g[1, 0], ln_b[1, 0])
    x2d = _peer(x2d, peer_w_q[1], peer_sub_keys[1], peer_u[1], peer_v[1], ln_g[1, 1], ln_b[1, 1])
    return x2d.reshape(B, S, D)
```

```python
import functools
import math

import jax
import jax.numpy as jnp
from jax import lax
from jax.experimental import pallas as pl
from jax.experimental.pallas import tpu as pltpu

F32 = jnp.float32
BF16 = jnp.bfloat16

DIFF_HEADS = 8
DIFF_HEAD_DIM = 64
SB_HEADS = 16
SB_HEAD_DIM = 64
PEER_HEADS = 8
PEER_N_KEYS = 128
PEER_TOPK = 16
PEER_HALF = 64
CHUNK = 64
ROPE_THETA = 10000.0
LN_EPS = 1e-5
RMS_EPS = 1e-5
DEPTH = 2
DEEPNORM_ALPHA = (2.0 * DEPTH) ** 0.25

LANES = 128
SUM_ROWS = 16
LOG2E = math.log2(math.e)

PROJ_TM = 256
ATTN_TQ = 512
ATTN_TK = 256
SB_TQ = 512
SB_TK = 256
ROUTE_TM = 256
DENSE_TM = 512
DENSE_TE = 1024
VMEM_LIMIT = 56 * 1024 * 1024


def _cparams(sem):
    return pltpu.CompilerParams(dimension_semantics=sem, vmem_limit_bytes=VMEM_LIMIT)


def _proj_kernel(x_ref, w_ref, cos_ref, sina_ref, sinb_ref, o_ref, *, rope_groups, scale_groups, scale):
    x = x_ref[...].astype(BF16)
    acc = jnp.dot(x, w_ref[...], preferred_element_type=F32)
    n_groups = acc.shape[1] // LANES
    for g in range(n_groups):
        t = acc[:, g * LANES:(g + 1) * LANES]
        if g < rope_groups:
            t = (t * cos_ref[...] + pltpu.roll(t, LANES - 32, 1) * sina_ref[...]
                 + pltpu.roll(t, 32, 1) * sinb_ref[...])
        if g < scale_groups:
            t = t * scale
        o_ref[:, g * LANES:(g + 1) * LANES] = t.astype(o_ref.dtype)


def _proj(x2d, w_bf16, tables, *, seq, rope_groups, scale_groups, scale):
    T, D = x2d.shape
    N = w_bf16.shape[1]
    tm = PROJ_TM
    n_seq_blocks = seq // tm
    cos, sina, sinb = tables
    tab_spec = pl.BlockSpec((tm, LANES), lambda i: (i % n_seq_blocks, 0))
    return pl.pallas_call(
        functools.partial(_proj_kernel, rope_groups=rope_groups, scale_groups=scale_groups, scale=scale),
        grid=(T // tm,),
        in_specs=[pl.BlockSpec((tm, D), lambda i: (i, 0)),
                  pl.BlockSpec((D, N), lambda i: (0, 0)),
                  tab_spec, tab_spec, tab_spec],
        out_specs=pl.BlockSpec((tm, N), lambda i: (i, 0)),
        out_shape=jax.ShapeDtypeStruct((T, N), BF16),
        compiler_params=_cparams(("parallel",)),
        name="proj",
    )(x2d, w_bf16, cos, sina, sinb)


def _rope_tables(seq):
    pos = jnp.arange(seq, dtype=F32)
    inv = ROPE_THETA ** (-jnp.arange(0, DIFF_HEAD_DIM, 2, dtype=F32) / DIFF_HEAD_DIM)
    ang = pos[:, None] * inv[None, :]
    ang = jnp.concatenate([ang, ang, ang, ang], axis=-1)
    cos, sin = jnp.cos(ang), jnp.sin(ang)
    first_half = (jnp.arange(LANES) % DIFF_HEAD_DIM) < (DIFF_HEAD_DIM // 2)
    sina = jnp.where(first_half[None, :], -sin, 0.0)
    sinb = jnp.where(first_half[None, :], 0.0, sin)
    return cos, sina, sinb


def _transpose_value_blocks(v_ref, vt_ref, tk):
    for c in range(vt_ref.shape[0]):
        vt_ref[c] = v_ref[c * tk:(c + 1) * tk, :].astype(F32).T.astype(BF16)


def _split_halves_t(q_ref):
    qt = q_ref[...].astype(F32).T
    sub = lax.broadcasted_iota(jnp.int32, qt.shape, 0)
    first = jnp.where(sub < LANES // 2, qt, 0.0).astype(BF16)
    second = jnp.where(sub >= LANES // 2, qt, 0.0).astype(BF16)
    return first, second, sub


def _diff_attn_kernel(lp_ref, g_ref, q_ref, k_ref, v_ref, o_ref, vt_ref, m_ref, l_ref, acc_ref, s_ref, *,
                      tq, tk, lambda_init):
    i = pl.program_id(2)

    @pl.when(i == 0)
    def _():
        _transpose_value_blocks(v_ref, vt_ref, tk)

    q_maps = _split_halves_t(q_ref)[:2]

    m_ref[...] = jnp.full(m_ref.shape, -jnp.inf, F32)
    l_ref[...] = jnp.zeros(l_ref.shape, F32)
    acc_ref[...] = jnp.zeros(acc_ref.shape, F32)

    def scores(j, mi):
        start = pl.multiple_of(j * tk, tk)
        return jnp.dot(k_ref[pl.ds(start, tk), :], q_maps[mi], preferred_element_type=F32)

    def block_softmax(s, j, masked):
        if masked:
            key = j * tk + lax.broadcasted_iota(jnp.int32, s.shape, 0)
            qry = i * tq + lax.broadcasted_iota(jnp.int32, s.shape, 1)
            s = jnp.where((key // CHUNK) <= (qry // CHUNK), s, -jnp.inf)
        m_blk = jnp.max(s, axis=0, keepdims=True)
        m_ref_pt = jnp.where(m_blk == -jnp.inf, 0.0, m_blk) if masked else m_blk
        p = jnp.exp2(s - m_ref_pt)
        l_blk = jnp.sum(p, axis=0, keepdims=True)
        pv = jnp.dot(vt_ref[j], p.astype(BF16), preferred_element_type=F32)
        return m_blk, l_blk, pv

    def combine(mi, blk):
        m_blk, l_blk, pv = blk
        m_prev = m_ref[mi]
        m_new = jnp.maximum(m_prev, m_blk)
        a_prev = jnp.exp2(m_prev - m_new)
        a_blk = jnp.exp2(m_blk - m_new)
        l_ref[mi] = a_prev * l_ref[mi] + a_blk * l_blk
        acc_ref[mi] = a_prev * acc_ref[mi] + a_blk * pv
        m_ref[mi] = m_new

    assert tq == 2 * tk
    chains = [(d, mi) for d in range(2) for mi in range(2)]

    for d, mi in chains:
        combine(mi, block_softmax(scores(2 * i + d, mi), 2 * i + d, True))

    @pl.when(i > 0)
    def _():
        for c, (d, mi) in enumerate(chains):
            s_ref[c] = scores(2 * (i - 1) + d, mi)

        def body(t, carry):
            j_cur = 2 * (i - 1 - t)
            for c, (d, mi) in enumerate(chains):
                blk = block_softmax(s_ref[c], j_cur + d, False)
                s_ref[c] = scores(j_cur - 2 + d, mi)
                combine(mi, blk)
            return carry

        lax.fori_loop(0, i - 1, body, 0)
        for c, (d, mi) in enumerate(chains):
            combine(mi, block_softmax(s_ref[c], d, False))

    lp = lp_ref[...]
    s01 = jnp.sum(lp[0:1, :] * lp[1:2, :], axis=1, keepdims=True)
    s23 = jnp.sum(lp[2:3, :] * lp[3:4, :], axis=1, keepdims=True)
    lam = jnp.exp(s01) - jnp.exp(s23) + lambda_init
    o = acc_ref[0] * (1.0 / l_ref[0]) - lam * (acc_ref[1] * (1.0 / l_ref[1]))
    o = o * lax.rsqrt(jnp.mean(o * o, axis=0, keepdims=True) + RMS_EPS)
    o = o * g_ref[...] * (1.0 - lambda_init)
    o_ref[...] = o.T.astype(o_ref.dtype)


def _diff_attention(qkv, lam_params, subln_g, *, batch, seq, lambda_init):
    T = qkv.shape[0]
    tq, tk = ATTN_TQ, ATTN_TK
    nq = seq // tq
    H = DIFF_HEADS
    return pl.pallas_call(
        functools.partial(_diff_attn_kernel, tq=tq, tk=tk, lambda_init=lambda_init),
        grid=(batch, H, nq),
        in_specs=[pl.BlockSpec((4, DIFF_HEAD_DIM), lambda b, h, i: (0, 0)),
                  pl.BlockSpec((LANES, 1), lambda b, h, i: (0, 0)),
                  pl.BlockSpec((tq, LANES), lambda b, h, i: (b * nq + i, h)),
                  pl.BlockSpec((seq, LANES), lambda b, h, i: (b, H + h)),
                  pl.BlockSpec((seq, LANES), lambda b, h, i: (b, 2 * H + h))],
        out_specs=pl.BlockSpec((tq, LANES), lambda b, h, i: (b * nq + i, h)),
        out_shape=jax.ShapeDtypeStruct((T, H * LANES), BF16),
        scratch_shapes=[pltpu.VMEM((seq // tk, LANES, tk), BF16),
                        pltpu.VMEM((2, 1, tq), F32), pltpu.VMEM((2, 1, tq), F32),
                        pltpu.VMEM((2, LANES, tq), F32), pltpu.VMEM((4, tk, tq), F32)],
        compiler_params=_cparams(("parallel", "parallel", "arbitrary")),
        name="diff_attn",
    )(lam_params, subln_g.reshape(LANES, 1), qkv, qkv, qkv)


def _stick_kernel(q_ref, k_ref, v_ref, o_ref, vt_ref, r_ref, acc_ref, z_ref, *, tq, tk):
    i = pl.program_id(2)

    @pl.when(i == 0)
    def _():
        _transpose_value_blocks(v_ref, vt_ref, tk)

    q_first, q_second, sub = _split_halves_t(q_ref)
    q_heads = (q_first, q_second)

    jj = lax.broadcasted_iota(jnp.int32, (tk + SUM_ROWS, tk), 0)
    kk = lax.broadcasted_iota(jnp.int32, (tk + SUM_ROWS, tk), 1)
    suffix = jnp.where((kk >= jj) | (jj >= tk), 1.0, 0.0).astype(BF16)

    r_ref[...] = jnp.zeros(r_ref.shape, F32)
    acc_ref[...] = jnp.zeros(acc_ref.shape, F32)

    def scores(jb, hd):
        start = pl.multiple_of(jb * tk, tk)
        return jnp.dot(k_ref[pl.ds(start, tk), :], q_heads[hd], preferred_element_type=F32)

    def visible(jb, shape):
        key = jb * tk + lax.broadcasted_iota(jnp.int32, shape, 0)
        qry = i * tq + lax.broadcasted_iota(jnp.int32, shape, 1)
        return key < qry

    def split_logs(z, keep):
        nz = -z
        lp = jnp.log2(1.0 + jnp.exp2(jnp.minimum(z, nz)))
        lsm = jnp.minimum(nz, 0.0) - lp
        if keep is not None:
            lsm = jnp.where(keep, lsm, 0.0)
        return lsm.astype(BF16)

    def suffix_sums(hi_lo):
        return jnp.dot(suffix, hi_lo, preferred_element_type=F32)

    def weighted_values(z, sums, jb, keep):
        a = jnp.exp2(z + sums[0:tk])
        if keep is not None:
            a = jnp.where(keep, a, 0.0)
        pv = jnp.dot(vt_ref[jb], a.astype(BF16), preferred_element_type=F32)
        return pv, sums[tk:tk + 1]

    def combine(hd, blk):
        pv, total = blk
        acc_ref[hd] = acc_ref[hd] + jnp.exp2(r_ref[hd]) * pv
        r_ref[hd] = r_ref[hd] + total

    assert tq == 2 * tk
    chains = [(d, hd) for d in range(2) for hd in range(2)]

    for d, hd in chains:
        jb = 2 * i + 1 - d
        z = scores(jb, hd)
        keep = visible(jb, z.shape)
        combine(hd, weighted_values(z, suffix_sums(split_logs(z, keep)), jb, keep))

    def pipelined_pair(jb_right, prefetch):
        hi_lo, sums, blk = {}, {}, {}

        def logs(c):
            hi_lo[c] = split_logs(z_ref[c], None)
            sums[c] = suffix_sums(hi_lo[c])

        def values(c):
            d, hd = chains[c]
            blk[c] = weighted_values(z_ref[c], sums[c], jb_right - d, None)
            if prefetch:
                z_ref[c] = scores(jb_right - 2 - d, hd)

        logs(0)
        logs(1)
        values(0)
        logs(2)
        values(1)
        logs(3)
        values(2)
        values(3)
        for c, (d, hd) in enumerate(chains):
            combine(hd, blk[c])

    @pl.when(i > 0)
    def _():
        for c, (d, hd) in enumerate(chains):
            z_ref[c] = scores(2 * i - 1 - d, hd)

        def body(t, carry):
            pipelined_pair(2 * i - 1 - 2 * t, True)
            return carry

        lax.fori_loop(0, i - 1, body, 0)
        pipelined_pair(1, False)

    out = jnp.where(sub < SB_HEAD_DIM, acc_ref[0], acc_ref[1])
    o_ref[...] = out.T.astype(o_ref.dtype)


def _stick_breaking(qkv, *, batch, seq):
    T = qkv.shape[0]
    tq, tk = SB_TQ, SB_TK
    nq = seq // tq
    P = SB_HEADS // 2
    return pl.pallas_call(
        functools.partial(_stick_kernel, tq=tq, tk=tk),
        grid=(batch, P, nq),
        in_specs=[pl.BlockSpec((tq, LANES), lambda b, p, i: (b * nq + i, p)),
                  pl.BlockSpec((seq, LANES), lambda b, p, i: (b, P + p)),
                  pl.BlockSpec((seq, LANES), lambda b, p, i: (b, 2 * P + p))],
        out_specs=pl.BlockSpec((tq, LANES), lambda b, p, i: (b * nq + i, p)),
        out_shape=jax.ShapeDtypeStruct((T, P * LANES), BF16),
        scratch_shapes=[pltpu.VMEM((seq // tk, LANES, tk), BF16),
                        pltpu.VMEM((2, 1, tq), F32), pltpu.VMEM((2, LANES, tq), F32),
                        pltpu.VMEM((4, tk, tq), F32)],
        compiler_params=_cparams(("parallel", "parallel", "arbitrary")),
        name="stick_breaking",
    )(qkv, qkv, qkv)


def _layer_norm_rows(y, g, b):
    mu = jnp.mean(y, axis=1, keepdims=True)
    d = y - mu
    var = jnp.mean(d * d, axis=1, keepdims=True)
    return d * lax.rsqrt(var + LN_EPS) * g + b


def _proj_ln_kernel(a_ref, w_ref, x_ref, g_ref, b_ref, o_ref):
    y = jnp.dot(a_ref[...], w_ref[...], preferred_element_type=F32) + DEEPNORM_ALPHA * x_ref[...]
    o_ref[...] = _layer_norm_rows(y, g_ref[...], b_ref[...])


def _proj_ln(a_bf16, w_bf16, x2d, g, b):
    T, D = x2d.shape
    tm = PROJ_TM
    return pl.pallas_call(
        _proj_ln_kernel,
        grid=(T // tm,),
        in_specs=[pl.BlockSpec((tm, D), lambda i: (i, 0)),
                  pl.BlockSpec((D, D), lambda i: (0, 0)),
                  pl.BlockSpec((tm, D), lambda i: (i, 0)),
                  pl.BlockSpec((1, D), lambda i: (0, 0)),
                  pl.BlockSpec((1, D), lambda i: (0, 0))],
        out_specs=pl.BlockSpec((tm, D), lambda i: (i, 0)),
        out_shape=jax.ShapeDtypeStruct((T, D), F32),
        compiler_params=_cparams(("parallel",)),
        name="proj_ln",
    )(a_bf16, w_bf16, x2d, g.reshape(1, D), b.reshape(1, D))


N_TOP = PEER_TOPK + 1


def _extract_top(scores, n, with_rank):
    work = scores
    vals = []
    rank = jnp.full(scores.shape, float(PEER_N_KEYS), F32) if with_rank else None
    for r in range(n):
        m = jnp.max(work, axis=0, keepdims=True)
        vals.append(m)
        hit = work == m
        if with_rank:
            rank = jnp.where(hit, float(r), rank)
        work = jnp.where(hit, -jnp.inf, work)
    return vals, rank


def _route_kernel(x_ref, w_ref, kp_ref, rb_ref, e2_ref, n_ref, a1_ref):
    x = x_ref[...].astype(BF16)
    qp = jnp.dot(x, w_ref[...], preferred_element_type=F32).astype(BF16)
    tm = x.shape[0]
    for h in range(PEER_HEADS):
        qh = qp[:, h * LANES:(h + 1) * LANES]
        nt = (((1,), (1,)), ((), ()))
        s1 = lax.dot_general(kp_ref[h, 0], qh, nt, preferred_element_type=F32)
        s2 = lax.dot_general(kp_ref[h, 1], qh, nt, preferred_element_type=F32)
        top1, _ = _extract_top(s1, N_TOP, False)
        top2, rank2 = _extract_top(s2, N_TOP, True)
        cands = [top1[i] + top2[j] for i in range(N_TOP) for j in range(N_TOP)
                 if (i + 1) * (j + 1) <= N_TOP]
        pad = (-len(cands)) % 8
        cands += [jnp.full((1, tm), -jnp.inf, F32)] * pad
        ctop, _ = _extract_top(jnp.concatenate(cands, axis=0), N_TOP, False)
        cut = 0.5 * (ctop[PEER_TOPK - 1] + ctop[PEER_TOPK])
        cmax = ctop[0]
        z = ctop[0] - cmax
        z = jnp.exp(z)
        for r in range(1, PEER_TOPK):
            z = z + jnp.exp(ctop[r] - cmax)
        count1 = jnp.zeros(s1.shape, F32)
        for j in range(PEER_TOPK):
            count1 = jnp.where(s1 >= cut - top2[j], float(j + 1), count1)
        rb_ref[h] = rank2.astype(BF16)
        e2_ref[h] = jnp.exp(s2 - top2[0]).astype(BF16)
        n_ref[h] = count1
        a1_ref[h] = jnp.exp(s1 - top1[0]) / z


def _peer_route(x2d, w_pq_bf16, keys_padded):
    T, D = x2d.shape
    tm = ROUTE_TM
    out_spec = pl.BlockSpec((PEER_HEADS, PEER_N_KEYS, tm), lambda i: (0, 0, i))
    sds = lambda dtype: jax.ShapeDtypeStruct((PEER_HEADS, PEER_N_KEYS, T), dtype)
    return pl.pallas_call(
        _route_kernel,
        grid=(T // tm,),
        in_specs=[pl.BlockSpec((tm, D), lambda i: (i, 0)),
                  pl.BlockSpec((D, PEER_HEADS * LANES), lambda i: (0, 0)),
                  pl.BlockSpec((PEER_HEADS, 2, PEER_N_KEYS, LANES), lambda i: (0, 0, 0, 0))],
        out_specs=[out_spec] * 4,
        out_shape=[sds(BF16), sds(BF16), sds(F32), sds(F32)],
        compiler_params=_cparams(("parallel",)),
        name="peer_route",
    )(x2d, w_pq_bf16, keys_padded)


def _peer_dense_kernel(x_ref, u_ref, vta_ref, vtb_ref, rb_ref, e2_ref, na_ref, a1a_ref, nb_ref, a1b_ref,
                       g_ref, b_ref, o_ref, xt_ref, acc_ref, ht_a_ref, ht_b_ref, gs_ref, *, te):
    s = pl.program_id(1)
    last = pl.num_programs(1) - 1

    @pl.when(s == 0)
    def _():
        xt_ref[...] = x_ref[...].T.astype(BF16)
        acc_ref[...] = jnp.zeros(acc_ref.shape, F32)
        ht_b_ref[...] = jnp.zeros(ht_b_ref.shape, F32)

    def half_step(half, n_ref, a1_ref, vt_ref, ht_out, ht_in, live):
        ht_out[...] = jnp.dot(u_ref[half * te:(half + 1) * te, :], xt_ref[...],
                              preferred_element_type=F32)
        for i in range(te // PEER_N_KEYS):
            rows = slice(i * PEER_N_KEYS, (i + 1) * PEER_N_KEYS)
            w = None
            for h in range(PEER_HEADS):
                cnt = n_ref[h, i:i + 1, :].astype(BF16)
                a1 = a1_ref[h, i:i + 1, :].astype(BF16)
                if live is not None:
                    a1 = a1 * live
                term = a1 * jnp.where(rb_ref[h] < cnt, e2_ref[h], 0)
                w = term if w is None else w + term
            hb = ht_in[rows, :].astype(BF16)
            gelu = (0.5 * hb) * (1.0 + lax.erf(hb * (1.0 / math.sqrt(2.0))))
            gs_ref[rows, :] = w * gelu
        acc_ref[...] += jnp.dot(vt_ref[...], gs_ref[...], preferred_element_type=F32)

    half_step(0, na_ref, a1a_ref, vta_ref, ht_a_ref, ht_b_ref, None)
    not_draining = jnp.where(s < last, 1.0, 0.0).astype(BF16)
    half_step(1, nb_ref, a1b_ref, vtb_ref, ht_b_ref, ht_a_ref, not_draining)

    @pl.when(s == last)
    def _():
        y = acc_ref[...].T + DEEPNORM_ALPHA * x_ref[...]
        o_ref[...] = _layer_norm_rows(y, g_ref[...], b_ref[...])


def _peer_dense(x2d, u_bf16, vt_bf16, routing, g, b):
    T, D = x2d.shape
    E = u_bf16.shape[0]
    tm, te = DENSE_TM, DENSE_TE
    n_tiles = E // te
    n_pairs = n_tiles // 2
    rb, e2, cnt, a1 = routing
    tile_a = lambda s: jnp.clip(2 * s - 1, 0, n_tiles - 1)
    tile_b = lambda s: jnp.clip(2 * s, 0, n_tiles - 1)
    full_spec = pl.BlockSpec((PEER_HEADS, PEER_N_KEYS, tm), lambda t, s: (0, 0, t))
    rows_a = pl.BlockSpec((PEER_HEADS, te // PEER_N_KEYS, tm), lambda t, s: (0, tile_a(s), t))
    rows_b = pl.BlockSpec((PEER_HEADS, te // PEER_N_KEYS, tm), lambda t, s: (0, tile_b(s), t))
    return pl.pallas_call(
        functools.partial(_peer_dense_kernel, te=te),
        grid=(T // tm, n_pairs + 1),
        in_specs=[pl.BlockSpec((tm, D), lambda t, s: (t, 0)),
                  pl.BlockSpec((2 * te, D), lambda t, s: (jnp.minimum(s, n_pairs - 1), 0)),
                  pl.BlockSpec((D, te), lambda t, s: (0, tile_a(s))),
                  pl.BlockSpec((D, te), lambda t, s: (0, tile_b(s))),
                  full_spec, full_spec, rows_a, rows_a, rows_b, rows_b,
                  pl.BlockSpec((1, D), lambda t, s: (0, 0)),
                  pl.BlockSpec((1, D), lambda t, s: (0, 0))],
        out_specs=pl.BlockSpec((tm, D), lambda t, s: (t, 0)),
        out_shape=jax.ShapeDtypeStruct((T, D), F32),
        scratch_shapes=[pltpu.VMEM((D, tm), BF16), pltpu.VMEM((D, tm), F32),
                        pltpu.VMEM((te, tm), F32), pltpu.VMEM((te, tm), F32),
                        pltpu.VMEM((te, tm), BF16)],
        compiler_params=_cparams(("parallel", "arbitrary")),
        name="peer_dense",
    )(x2d, u_bf16, vt_bf16, vt_bf16, rb, e2, cnt, a1, cnt, a1, g.reshape(1, D), b.reshape(1, D))


def _pad_sub_keys(sub_keys):
    z = jnp.zeros_like(sub_keys[:, 0])
    first = jnp.concatenate([sub_keys[:, 0], z], axis=-1)
    second = jnp.concatenate([z, sub_keys[:, 1]], axis=-1)
    return jnp.stack([first, second], axis=1).astype(BF16)


def _peer(x2d, w_pq, sub_keys, u, v, g, b):
    routing = _peer_route(x2d, w_pq.astype(BF16), _pad_sub_keys(sub_keys))
    return _peer_dense(x2d, u.astype(BF16), v.T.astype(BF16), routing, g, b)


def kernel(x, ln_g, ln_b, w_qkv_a, w_o_a, lambda_qk_a, subln_g_a, w_kv_b, w_q_b, w_o_b,
           peer_w_q, peer_sub_keys, peer_u, peer_v):
    B, S, D = x.shape
    x2d = x.reshape(B * S, D)
    tables = _rope_tables(S)
    n_qk_groups = 2 * D // LANES

    lambda_init = 0.8 - 0.6 * math.exp(-0.3 * 0)
    qkv = _proj(x2d, w_qkv_a[0].astype(BF16), tables, seq=S, rope_groups=n_qk_groups,
                scale_groups=D // LANES, scale=DIFF_HEAD_DIM ** -0.5 * LOG2E)
    att = _diff_attention(qkv, lambda_qk_a[0], subln_g_a[0], batch=B, seq=S, lambda_init=lambda_init)
    x2d = _proj_ln(att, w_o_a[0].astype(BF16), x2d, ln_g[0, 0], ln_b[0, 0])
    x2d = _peer(x2d, peer_w_q[0], peer_sub_keys[0], peer_u[0], peer_v[0], ln_g[0, 1], ln_b[0, 1])

    w_qkv_b = jnp.concatenate([w_q_b[0], w_kv_b], axis=1).astype(BF16)
    qkv = _proj(x2d, w_qkv_b, tables, seq=S, rope_groups=0,
                scale_groups=D // LANES, scale=SB_HEAD_DIM ** -0.5 * LOG2E)
    sb = _stick_breaking(qkv, batch=B, seq=S)
    x2d = _proj_ln(sb, w_o_b[0].astype(BF16), x2d, ln_g[1, 0], ln_b[1, 0])
    x2d = _peer(x2d, peer_w_q[1], peer_sub_keys[1], peer_u[1], peer_v[1], ln_g[1, 1], ln_b[1, 1])
    return x2d.reshape(B, S, D)
```
